```python
import math
import jax, jax.numpy as jnp
from jax import lax
import numpy as np

D_MODEL = 2048
BATCH = 2
SEQ = 4096
DEPTH = 1

GRID_W = 64
CTX_LEN = 256
GLA_HEADS = 8
GLA_DK = 64
GLA_DV = 128
GLA_RANK = 16
GLA_TAU = 16.0
GLA_CHUNK = 64
DIFF_HEADS = 8
DIFF_DQK = 64
DIFF_DV = 128
D_FF = 5632
CONV_W = 3
Q_BLOCK = 128
ROPE_THETA = 10000.0
EPS = 1e-6

GLA_QK = GLA_HEADS * GLA_DK
GLA_V = GLA_HEADS * GLA_DV
DIFF_QK = DIFF_HEADS * 2 * DIFF_DQK
DIFF_V = DIFF_HEADS * DIFF_DV
IN_SIZES = [GLA_QK, GLA_QK, GLA_V, GLA_V, GLA_RANK, GLA_RANK, DIFF_QK, DIFF_QK, DIFF_V]
D_IN = sum(IN_SIZES)
IN_OFFSETS = [int(v) for v in np.cumsum(IN_SIZES)[:-1]]

kernel_name = "hybrid_gla_diffattn_convffn_dit_block"


def rms_norm(x, w):
    xf = x.astype(jnp.float32)
    y = xf * lax.rsqrt(jnp.mean(xf * xf, axis=-1, keepdims=True) + EPS)
    return (y * w.astype(jnp.float32)).astype(x.dtype)


def modulate(h, shift, scale):
    return h * (1.0 + scale) + shift


def to_heads(t, n, d):
    B, T, _ = t.shape
    return t.reshape(B, T, n, d).transpose(0, 2, 1, 3)


def from_heads(t):
    B, n, T, d = t.shape
    return t.transpose(0, 2, 1, 3).reshape(B, T, n * d)


def rope_axis(x, ang):
    half = x.shape[-1] // 2
    x1, x2 = x[..., :half], x[..., half:]
    cos = jnp.cos(ang).astype(x.dtype)
    sin = jnp.sin(ang).astype(x.dtype)
    return jnp.concatenate([x1 * cos - x2 * sin, x2 * cos + x1 * sin], axis=-1)


def rope_2d(x, ang_row, ang_col):
    r = x.shape[-1] // 2
    return jnp.concatenate([rope_axis(x[..., :r], ang_row), rope_axis(x[..., r:], ang_col)], axis=-1)


def gla_scan(q, k, v, log_a, s0):
    B, H, T, dk = q.shape
    dv = v.shape[-1]
    n = T // GLA_CHUNK

    def chunks(t):
        return jnp.moveaxis(t.astype(jnp.float32).reshape(B, H, n, GLA_CHUNK, t.shape[-1]), 2, 0)

    lower = jnp.tril(jnp.ones((GLA_CHUNK, GLA_CHUNK), dtype=bool))[:, :, None]

    def step(state, inp):
        qc, kc, vc, ac = inp
        b = jnp.cumsum(ac, axis=2)
        o_inter = jnp.einsum('bhid,bhde->bhie', qc * jnp.exp(b), state)
        rel = b[:, :, :, None, :] - b[:, :, None, :, :]
        decay = jnp.exp(jnp.where(lower, rel, -jnp.inf))
        scores = jnp.einsum('bhid,bhjd,bhijd->bhij', qc, kc, decay)
        o_intra = jnp.einsum('bhij,bhje->bhie', scores, vc)
        b_last = b[:, :, -1:, :]
        state = jnp.exp(b_last[:, :, 0, :])[..., None] * state + jnp.einsum(
            'bhjd,bhje->bhde', kc * jnp.exp(b_last - b), vc)
        return state, o_inter + o_intra

    s_final, o = lax.scan(step, s0.astype(jnp.float32), (chunks(q), chunks(k), chunks(v), chunks(log_a)))
    o = jnp.moveaxis(o, 0, 2).reshape(B, H, T, dv)
    return o.astype(v.dtype), s_final


def diff_softmax(q, k, v, lam):
    s = jnp.einsum('bhmqd,bhmkd->bhmqk', q, k).astype(jnp.float32) * (DIFF_DQK ** -0.5)
    p = jax.nn.softmax(s, axis=-1)
    a = p[:, :, 0] - lam * p[:, :, 1]
    return jnp.einsum('bhqk,bhkd->bhqd', a.astype(v.dtype), v)


def conv_ffn(h, w_up, conv_w, conv_b, w_down):
    u = h @ w_up
    T = u.shape[1]
    pad = CONV_W // 2
    up = jnp.pad(u, ((0, 0), (pad, pad), (0, 0)))
    u = sum(up[:, j:j + T] * conv_w[j] for j in range(CONV_W)) + conv_b
    gate, val = jnp.split(u, 2, axis=-1)
    return (jax.nn.silu(gate) * val) @ w_down


def hybrid_layer(x, ctx, c, c_ctx, w_ada, b_ada, norm1_w, w_in, w_a_up_f, b_a_f, w_a_up_b, b_a_b,
                 gla_onorm_w, diff_qnorm_w, diff_knorm_w, lambda_q1, lambda_k1, lambda_q2, lambda_k2,
                 diff_onorm_w, w_proj_gla, w_proj_diff, w_gate, b_gate, w_out, norm2_w, w_up, conv_w,
                 conv_b, w_down, ang_row, ang_col, lam_init, update_ctx):
    B, S, _ = x.shape
    sh1, sc1, g1, sh2, sc2, g2 = [m[:, None, :] for m in
                                  jnp.split(jax.nn.silu(c) @ w_ada + b_ada, 6, axis=-1)]
    sh1c, sc1c, g1c, sh2c, sc2c, g2c = jnp.split(jax.nn.silu(c_ctx) @ w_ada + b_ada, 6, axis=-1)

    h_l = modulate(rms_norm(x, norm1_w), sh1, sc1)
    h_c = modulate(rms_norm(ctx, norm1_w), sh1c, sc1c)
    gq_l, gk_l, gv_l, gr_l, gaf_l, gab_l, dq_l, dk_l, dv_l = jnp.split(h_l @ w_in, IN_OFFSETS, axis=-1)
    gq_c, gk_c, gv_c, gr_c, gaf_c, gab_c, dq_c, dk_c, dv_c = jnp.split(h_c @ w_in, IN_OFFSETS, axis=-1)

    def gla_heads(gq, gk, gv, gaf, gab):
        q = to_heads(gq, GLA_HEADS, GLA_DK) * (GLA_DK ** -0.5)
        k = to_heads(gk, GLA_HEADS, GLA_DK)
        v = to_heads(gv, GLA_HEADS, GLA_DV)
        la_f = to_heads(jax.nn.log_sigmoid((gaf @ w_a_up_f + b_a_f).astype(jnp.float32)) / GLA_TAU,
                        GLA_HEADS, GLA_DK)
        la_b = to_heads(jax.nn.log_sigmoid((gab @ w_a_up_b + b_a_b).astype(jnp.float32)) / GLA_TAU,
                        GLA_HEADS, GLA_DK)
        return q, k, v, la_f, la_b

    aq_l, ak_l, av_l, alf_l, alb_l = gla_heads(gq_l, gk_l, gv_l, gaf_l, gab_l)
    aq_c, ak_c, av_c, alf_c, alb_c = gla_heads(gq_c, gk_c, gv_c, gaf_c, gab_c)
    zeros = jnp.zeros((B, GLA_HEADS, GLA_DK, GLA_DV), jnp.float32)

    def flip(t):
        return jnp.flip(t, axis=2)

    o_cf, s_cf = gla_scan(aq_c, ak_c, av_c, alf_c, zeros)
    o_lf, _ = gla_scan(aq_l, ak_l, av_l, alf_l, s_cf)
    o_cb, s_cb = gla_scan(flip(aq_c), flip(ak_c), flip(av_c), flip(alb_c), zeros)
    o_lb, _ = gla_scan(flip(aq_l), flip(ak_l), flip(av_l), flip(alb_l), s_cb)

    def gla_out(o, r):
        return from_heads(rms_norm(o, gla_onorm_w)) * jax.nn.silu(r)

    def diff_qk(t, w):
        Bt, T, _ = t.shape
        t = t.reshape(Bt, T, DIFF_HEADS, 2, DIFF_DQK).transpose(0, 2, 3, 1, 4)
        return rms_norm(t, w)

    q_l = rope_2d(diff_qk(dq_l, diff_qnorm_w), ang_row, ang_col)
    k_l = rope_2d(diff_qk(dk_l, diff_knorm_w), ang_row, ang_col)
    k_c = diff_qk(dk_c, diff_knorm_w)
    v_l = to_heads(dv_l, DIFF_HEADS, DIFF_DV)
    v_c = to_heads(dv_c, DIFF_HEADS, DIFF_DV)
    lam = (jnp.exp(jnp.sum(lambda_q1.astype(jnp.float32) * lambda_k1.astype(jnp.float32)))
           - jnp.exp(jnp.sum(lambda_q2.astype(jnp.float32) * lambda_k2.astype(jnp.float32))) + lam_init)

    k_all = jnp.concatenate([k_c, k_l], axis=3)
    v_all = jnp.concatenate([v_c, v_l], axis=2)
    nb = S // Q_BLOCK
    q_blocks = jnp.moveaxis(q_l.reshape(B, DIFF_HEADS, 2, nb, Q_BLOCK, DIFF_DQK), 3, 0)
    o_diff = lax.map(lambda qb: diff_softmax(qb, k_all, v_all, lam), q_blocks)
    o_diff = jnp.moveaxis(o_diff, 0, 2).reshape(B, DIFF_HEADS, S, DIFF_DV)

    def diff_out(o):
        return from_heads(rms_norm(o, diff_onorm_w) * (1.0 - lam_init))

    def merge(h, ya, yb):
        g_a, g_b = jnp.split(jax.nn.sigmoid(h @ w_gate + b_gate), 2, axis=-1)
        return (g_a * (ya @ w_proj_gla) + g_b * (yb @ w_proj_diff)) @ w_out

    mix_l = merge(h_l, gla_out(o_lf + flip(o_lb), gr_l), diff_out(o_diff))
    x_new = x + g1 * mix_l
    h2_l = modulate(rms_norm(x_new, norm2_w), sh2, sc2)
    x_new = x_new + g2 * conv_ffn(h2_l, w_up, conv_w, conv_b, w_down)

    if update_ctx:
        q_c = diff_qk(dq_c, diff_qnorm_w)
        o_diff_c = diff_softmax(q_c, k_c, v_c, lam)
        mix_c = merge(h_c, gla_out(o_cf + flip(o_cb), gr_c), diff_out(o_diff_c))
        ctx = ctx + g1c * mix_c
        h2_c = modulate(rms_norm(ctx, norm2_w), sh2c, sc2c)
        ctx = ctx + g2c * conv_ffn(h2_c, w_up, conv_w, conv_b, w_down)
    return x_new, ctx


def setup_inputs(seed: int = 0) -> dict:
    key = jax.random.key(seed)
    ks = jax.random.split(key, 32)
    f32 = jnp.float32
    D = D_MODEL

    def nrm(k, shape, scale):
        return jax.random.normal(k, shape, f32) * scale

    def gain(k, shape):
        return 1.0 + 0.02 * jax.random.normal(k, shape, f32)

    return {
        "x": nrm(ks[0], (BATCH, SEQ, D), 1.0),
        "c": nrm(ks[1], (BATCH, D), 1.0),
        "ctx": nrm(ks[2], (BATCH, CTX_LEN, D), 1.0),
        "c_ctx": nrm(ks[3], (D,), 1.0),
        "w_ada": nrm(ks[4], (DEPTH, D, 6 * D), D ** -0.5),
        "b_ada": nrm(ks[5], (DEPTH, 6 * D), 0.01),
        "norm1_w": gain(ks[6], (DEPTH, D)),
        "w_in": nrm(ks[7], (DEPTH, D, D_IN), D ** -0.5),
        "w_a_up_f": nrm(ks[8], (DEPTH, GLA_RANK, GLA_QK), GLA_RANK ** -0.5),
        "b_a_f": 1.0 + nrm(ks[9], (DEPTH, GLA_QK), 0.1),
        "w_a_up_b": nrm(ks[10], (DEPTH, GLA_RANK, GLA_QK), GLA_RANK ** -0.5),
        "b_a_b": 1.0 + nrm(ks[11], (DEPTH, GLA_QK), 0.1),
        "gla_onorm_w": gain(ks[12], (DEPTH, GLA_DV)),
        "diff_qnorm_w": gain(ks[13], (DEPTH, DIFF_DQK)),
        "diff_knorm_w": gain(ks[14], (DEPTH, DIFF_DQK)),
        "lambda_q1": nrm(ks[15], (DEPTH, DIFF_DQK), 0.1),
        "lambda_k1": nrm(ks[16], (DEPTH, DIFF_DQK), 0.1),
        "lambda_q2": nrm(ks[17], (DEPTH, DIFF_DQK), 0.1),
        "lambda_k2": nrm(ks[18], (DEPTH, DIFF_DQK), 0.1),
        "diff_onorm_w": gain(ks[19], (DEPTH, DIFF_DV)),
        "w_proj_gla": nrm(ks[20], (DEPTH, GLA_V, D), GLA_V ** -0.5),
        "w_proj_diff": nrm(ks[21], (DEPTH, DIFF_V, D), DIFF_V ** -0.5),
        "w_gate": nrm(ks[22], (DEPTH, D, 2 * D), D ** -0.5),
        "b_gate": nrm(ks[23], (DEPTH, 2 * D), 0.01),
        "w_out": nrm(ks[24], (DEPTH, D, D), D ** -0.5),
        "norm2_w": gain(ks[25], (DEPTH, D)),
        "w_up": nrm(ks[26], (DEPTH, D, 2 * D_FF), D ** -0.5),
        "conv_w": nrm(ks[27], (DEPTH, CONV_W, 2 * D_FF), CONV_W ** -0.5),
        "conv_b": nrm(ks[28], (DEPTH, 2 * D_FF), 0.01),
        "w_down": nrm(ks[29], (DEPTH, D_FF, D), D_FF ** -0.5),
    }


def reference(x, c, ctx, c_ctx, w_ada, b_ada, norm1_w, w_in, w_a_up_f, b_a_f, w_a_up_b, b_a_b,
              gla_onorm_w, diff_qnorm_w, diff_knorm_w, lambda_q1, lambda_k1, lambda_q2, lambda_k2,
              diff_onorm_w, w_proj_gla, w_proj_diff, w_gate, b_gate, w_out, norm2_w, w_up, conv_w,
              conv_b, w_down):
    S = x.shape[1]
    ROWS = S // GRID_W
    pos_row = jnp.repeat(jnp.arange(ROWS), GRID_W).astype(jnp.float32)
    pos_col = jnp.tile(jnp.arange(GRID_W), ROWS).astype(jnp.float32)
    n_freq = DIFF_DQK // 4
    inv_freq = ROPE_THETA ** (-jnp.arange(n_freq, dtype=jnp.float32) / n_freq)
    ang_row = pos_row[:, None] * inv_freq
    ang_col = pos_col[:, None] * inv_freq
    for i in range(DEPTH):
        lam_init = 0.8 - 0.6 * math.exp(-0.3 * i)
        x, ctx = hybrid_layer(
            x, ctx, c, c_ctx, w_ada[i], b_ada[i], norm1_w[i], w_in[i], w_a_up_f[i], b_a_f[i],
            w_a_up_b[i], b_a_b[i], gla_onorm_w[i], diff_qnorm_w[i], diff_knorm_w[i], lambda_q1[i],
            lambda_k1[i], lambda_q2[i], lambda_k2[i], diff_onorm_w[i], w_proj_gla[i], w_proj_diff[i],
            w_gate[i], b_gate[i], w_out[i], norm2_w[i], w_up[i], conv_w[i], conv_b[i], w_down[i],
            ang_row, ang_col, lam_init, i < DEPTH - 1)
    return x
```

```python
import functools
import math

import jax
import jax.numpy as jnp
import numpy as np
from jax import lax
from jax.experimental import pallas as pl
from jax.experimental.pallas import tpu as pltpu

F32 = jnp.float32
BF16 = jnp.bfloat16

GRID_W = 64
GLA_HEADS = 8
GLA_DK = 64
GLA_DV = 128
GLA_RANK = 16
GLA_TAU = 16.0
DIFF_HEADS = 8
DIFF_DQK = 64
DIFF_DV = 128
CONV_W = 3
ROPE_THETA = 10000.0
EPS = 1e-6
LAM_INIT = 0.8 - 0.6 * math.exp(-0.3 * 0)

GLA_QK = GLA_HEADS * GLA_DK
GLA_V = GLA_HEADS * GLA_DV
DIFF_QK = DIFF_HEADS * 2 * DIFF_DQK
DIFF_V = DIFF_HEADS * DIFF_DV

LANES = 128
VMEM_LIMIT = 56 * 1024 * 1024

GLA_CHUNK = 64
GLA_REFS = 4
LOG2E = 1.4426950408889634


def _params(n_axes):
    return pltpu.CompilerParams(dimension_semantics=("arbitrary",) * n_axes,
                                vmem_limit_bytes=VMEM_LIMIT)


def _sigmoid(x):
    return 1.0 / (1.0 + jnp.exp(-x))


def _split_bf16(x):
    hi = x.astype(BF16)
    lo = (x - hi.astype(F32)).astype(BF16)
    return hi, lo


def _ada_kernel(c_ref, w_ref, b_ref, o_ref):
    c = c_ref[...]
    s = (c * _sigmoid(c)).astype(BF16)
    o_ref[...] = jnp.dot(s, w_ref[...].astype(BF16), preferred_element_type=F32) + b_ref[...]


def _ada(cs, w, b):
    d, n = w.shape
    tn = 1024
    return pl.pallas_call(
        _ada_kernel,
        grid=(n // tn,),
        in_specs=[pl.BlockSpec((8, d), lambda j: (0, 0)),
                  pl.BlockSpec((d, tn), lambda j: (0, j)),
                  pl.BlockSpec((1, tn), lambda j: (0, j))],
        out_specs=pl.BlockSpec((8, tn), lambda j: (0, j)),
        out_shape=jax.ShapeDtypeStruct((8, n), F32),
        compiler_params=_params(1),
        name="ada",
    )(cs, w, b.reshape(1, n))


def _norm_mod(x, nw, sh, sc):
    ms = jnp.mean(x * x, axis=-1, keepdims=True)
    y = x * lax.rsqrt(ms + EPS) * nw
    return (y * (1.0 + sc) + sh).astype(BF16)


def _norm1_kernel(x_ref, ctx_ref, nw_ref, sh_ref, sc_ref, o_ref, *, n_lat_tiles):
    i = pl.program_id(0)

    @pl.when(i < n_lat_tiles)
    def _():
        o_ref[...] = _norm_mod(x_ref[...], nw_ref[...], sh_ref[...], sc_ref[...])

    @pl.when(i >= n_lat_tiles)
    def _():
        o_ref[...] = _norm_mod(ctx_ref[...], nw_ref[...], sh_ref[...], sc_ref[...])


def _norm1(x2, ctx2, nw, mod3, *, seq, n_batch, tm=512):
    rows, d = x2.shape
    crow = ctx2.shape[0]
    assert crow == tm and rows % tm == 0 and seq % tm == 0
    nl = rows // tm
    per_b = seq // tm

    def mod_row(i):
        return jnp.where(i < nl, i // per_b, n_batch)

    return pl.pallas_call(
        functools.partial(_norm1_kernel, n_lat_tiles=nl),
        grid=(nl + 1,),
        in_specs=[pl.BlockSpec((tm, d), lambda i: (jnp.minimum(i, nl - 1), 0)),
                  pl.BlockSpec((tm, d), lambda i: (0, 0)),
                  pl.BlockSpec((1, d), lambda i: (0, 0)),
                  pl.BlockSpec((None, 1, d), lambda i: (mod_row(i), 0, 0)),
                  pl.BlockSpec((None, 1, d), lambda i: (mod_row(i), 0, 1))],
        out_specs=pl.BlockSpec((tm, d), lambda i: (i, 0)),
        out_shape=jax.ShapeDtypeStruct((rows + crow, d), BF16),
        compiler_params=_params(1),
        name="norm1",
    )(x2, ctx2, nw.reshape(1, d), mod3, mod3)


def _norm2_kernel(x_ref, nw_ref, sh_ref, sc_ref, o_ref):
    o_ref[...] = _norm_mod(x_ref[...], nw_ref[...], sh_ref[...], sc_ref[...])


def _norm2(x2, nw, mod3, *, seq, tm=512):
    rows, d = x2.shape
    per_b = seq // tm
    return pl.pallas_call(
        _norm2_kernel,
        grid=(rows // tm,),
        in_specs=[pl.BlockSpec((tm, d), lambda i: (i, 0)),
                  pl.BlockSpec((1, d), lambda i: (0, 0)),
                  pl.BlockSpec((None, 1, d), lambda i: (i // per_b, 0, 3)),
                  pl.BlockSpec((None, 1, d), lambda i: (i // per_b, 0, 4))],
        out_specs=pl.BlockSpec((tm, d), lambda i: (i, 0)),
        out_shape=jax.ShapeDtypeStruct((rows, d), BF16),
        compiler_params=_params(1),
        name="norm2",
    )(x2, nw.reshape(1, d), mod3, mod3)


def _mm_kernel(x_ref, w_ref, o_ref):
    o_ref[...] = jnp.dot(x_ref[...], w_ref[...], preferred_element_type=F32).astype(o_ref.dtype)


def _mm(x, w, *, rows, tm, tn, out_dtype, name):
    k, n = w.shape
    return pl.pallas_call(
        _mm_kernel,
        grid=(n // tn, rows // tm),
        in_specs=[pl.BlockSpec((tm, k), lambda j, i: (i, 0)),
                  pl.BlockSpec((k, tn), lambda j, i: (0, j))],
        out_specs=pl.BlockSpec((tm, tn), lambda j, i: (i, j)),
        out_shape=jax.ShapeDtypeStruct((rows, n), out_dtype),
        compiler_params=_params(2),
        name=name,
    )(x, w)


def _mm_gate_kernel(x_ref, w_ref, b_ref, o_ref):
    z = jnp.dot(x_ref[...], w_ref[...], preferred_element_type=F32) + b_ref[...]
    o_ref[...] = _sigmoid(z).astype(o_ref.dtype)


def _mm_gate(x, w, b, *, rows, tm=1024, tn=1024):
    k, n = w.shape
    return pl.pallas_call(
        _mm_gate_kernel,
        grid=(n // tn, rows // tm),
        in_specs=[pl.BlockSpec((tm, k), lambda j, i: (i, 0)),
                  pl.BlockSpec((k, tn), lambda j, i: (0, j)),
                  pl.BlockSpec((1, tn), lambda j, i: (0, j))],
        out_specs=pl.BlockSpec((tm, tn), lambda j, i: (i, j)),
        out_shape=jax.ShapeDtypeStruct((rows, n), BF16),
        compiler_params=_params(2),
        name="gate",
    )(x, w, b.reshape(1, n))


def _vt_kernel(w_ref, x_ref, o_ref):
    o_ref[...] = lax.dot_general(w_ref[...], x_ref[...], (((1,), (1,)), ((), ())),
                                 preferred_element_type=F32).astype(o_ref.dtype)


def _mm_vt(wt, x, *, tn=256):
    n_out, k = wt.shape
    rows = x.shape[0]
    return pl.pallas_call(
        _vt_kernel,
        grid=(rows // tn,),
        in_specs=[pl.BlockSpec((n_out, k), lambda j: (0, 0)),
                  pl.BlockSpec((tn, k), lambda j: (j, 0))],
        out_specs=pl.BlockSpec((None, n_out, tn), lambda j: (j, 0, 0)),
        out_shape=jax.ShapeDtypeStruct((rows // tn, n_out, tn), BF16),
        compiler_params=_params(1),
        name="v_transposed",
    )(wt, x)


def _merge_kernel(ya_ref, yb_ref, wa_ref, wb_ref, ga_ref, gb_ref, o_ref):
    pa = jnp.dot(ya_ref[...], wa_ref[...], preferred_element_type=F32)
    pb = jnp.dot(yb_ref[...], wb_ref[...], preferred_element_type=F32)
    o_ref[...] = (ga_ref[...].astype(F32) * pa + gb_ref[...].astype(F32) * pb).astype(o_ref.dtype)


def _merge(ya, yb, wa, wb, g, *, tm=1024, tn=1024):
    rows, ka = ya.shape
    kb = yb.shape[1]
    n = wa.shape[1]
    nb = n // tn
    return pl.pallas_call(
        _merge_kernel,
        grid=(nb, rows // tm),
        in_specs=[pl.BlockSpec((tm, ka), lambda j, i: (i, 0)),
                  pl.BlockSpec((tm, kb), lambda j, i: (i, 0)),
                  pl.BlockSpec((ka, tn), lambda j, i: (0, j)),
                  pl.BlockSpec((kb, tn), lambda j, i: (0, j)),
                  pl.BlockSpec((tm, tn), lambda j, i: (i, j)),
                  pl.BlockSpec((tm, tn), lambda j, i: (i, nb + j))],
        out_specs=pl.BlockSpec((tm, tn), lambda j, i: (i, j)),
        out_shape=jax.ShapeDtypeStruct((rows, n), BF16),
        compiler_params=_params(2),
        name="merge",
    )(ya, yb, wa, wb, g, g)


def _resid_kernel(a_ref, w_ref, x_ref, g_ref, o_ref, acc_ref, *, nk):
    kk = pl.program_id(2)
    part = jnp.dot(a_ref[...], w_ref[...], preferred_element_type=F32)

    if nk == 1:
        o_ref[...] = x_ref[...] + g_ref[...] * part
    else:
        @pl.when(kk == 0)
        def _():
            acc_ref[...] = part

        @pl.when(jnp.logical_and(kk > 0, kk < nk - 1))
        def _():
            acc_ref[...] += part

        @pl.when(kk == nk - 1)
        def _():
            o_ref[...] = x_ref[...] + g_ref[...] * (acc_ref[...] + part)


def _resid_mm(a, w, xres, mod3, *, gate_chunk, seq, tm, tn, tk, name):
    rows, k = a.shape
    n = w.shape[1]
    nk = k // tk
    per_b = seq // tm
    cpc = n // tn
    return pl.pallas_call(
        functools.partial(_resid_kernel, nk=nk),
        grid=(n // tn, rows // tm, nk),
        in_specs=[pl.BlockSpec((tm, tk), lambda j, i, kk: (i, kk)),
                  pl.BlockSpec((tk, tn), lambda j, i, kk: (kk, j)),
                  pl.BlockSpec((tm, tn), lambda j, i, kk: (i, j)),
                  pl.BlockSpec((None, 1, tn), lambda j, i, kk: (i // per_b, 0, gate_chunk * cpc + j))],
        out_specs=pl.BlockSpec((tm, tn), lambda j, i, kk: (i, j)),
        out_shape=jax.ShapeDtypeStruct((rows, n), F32),
        scratch_shapes=[pltpu.VMEM((tm, tn), F32)],
        compiler_params=_params(3),
        name=name,
    )(a, w, xres, mod3)


def _decay_kernel(h_ref, wa_ref, wup_ref, bup_ref, tri_f_ref, tri_b_ref, o_ref, *, half):
    ga = jnp.dot(h_ref[...], wa_ref[...], preferred_element_type=F32)
    z = jnp.dot(ga.astype(BF16), wup_ref[...], preferred_element_type=F32) + bup_ref[...]
    la = (jnp.minimum(z, 0.0) - jnp.log1p(jnp.exp(-jnp.abs(z)))) * (1.0 / GLA_TAU)
    hi, lo = _split_bf16(la)
    for tri_ref, sl in ((tri_f_ref, slice(0, half)), (tri_b_ref, slice(half, 2 * half))):
        tri = tri_ref[...]
        o_ref[:, sl] = (jnp.dot(tri, hi[:, sl], preferred_element_type=F32)
                        + jnp.dot(tri, lo[:, sl], preferred_element_type=F32))


def _decay(h, wa, wup, bup, *, tm=512):
    rows, k = h.shape
    n = wup.shape[1]
    r = np.arange(tm)
    same = (r[:, None] // GLA_CHUNK) == (r[None, :] // GLA_CHUNK)
    tri_f = jnp.asarray(same & (r[None, :] <= r[:, None]), BF16)
    tri_b = jnp.asarray(same & (r[None, :] >= r[:, None]), BF16)
    return pl.pallas_call(
        functools.partial(_decay_kernel, half=n // 2),
        grid=(rows // tm,),
        in_specs=[pl.BlockSpec((tm, k), lambda i: (i, 0)),
                  pl.BlockSpec((k, LANES), lambda i: (0, 0)),
                  pl.BlockSpec((LANES, n), lambda i: (0, 0)),
                  pl.BlockSpec((1, n), lambda i: (0, 0)),
                  pl.BlockSpec((tm, tm), lambda i: (0, 0)),
                  pl.BlockSpec((tm, tm), lambda i: (0, 0))],
        out_specs=pl.BlockSpec((tm, n), lambda i: (i, 0)),
        out_shape=jax.ShapeDtypeStruct((rows, n), F32),
        compiler_params=_params(1),
        name="decay",
    )(h, wa, wup, bup, tri_f, tri_b)


def _gla_chunk(q, k, v, b, state_t, *, reverse):
    c = GLA_CHUNK
    bs = c // GLA_REFS
    lane = lax.broadcasted_iota(jnp.int32, (c, LANES), 1)
    head_a = lane < GLA_DK
    b_last = b[0:1, :] if reverse else b[c - 1:c, :]

    q_inter = q * jnp.exp(b)
    k_state = (k * jnp.exp(b_last - b)).astype(BF16)

    def rows_padded(t, lo):
        parts = [jnp.zeros((n, LANES), F32) for n in (lo,) if n] + [t]
        parts += [jnp.zeros((n, LANES), F32) for n in (c - lo - t.shape[0],) if n]
        return parts[0] if len(parts) == 1 else jnp.concatenate(parts, axis=0)

    q_segs, k_segs = [], []
    for r in range(GLA_REFS):
        lo, hi = r * bs, (r + 1) * bs
        m_r = b[lo:lo + 1, :] if reverse else b[hi - 1:hi, :]
        k_segs.append(rows_padded(k[lo:hi] * jnp.exp(m_r - b[lo:hi]), lo))
        if reverse:
            q_segs.append(rows_padded(q[:hi] * jnp.exp(b[:hi] - m_r), 0))
        else:
            q_segs.append(rows_padded(q[lo:] * jnp.exp(b[lo:] - m_r), lo))

    def per_head(t):
        ln = lax.broadcasted_iota(jnp.int32, t.shape, 1)
        ma = (ln % LANES) < GLA_DK
        return jnp.concatenate([jnp.where(ma, t, 0.0), jnp.where(ma, 0.0, t)], axis=0).astype(BF16)

    q_stack = per_head(jnp.concatenate(q_segs, axis=1))
    k_stack = jnp.concatenate(k_segs, axis=1).astype(BF16)
    scores = lax.dot_general(q_stack, k_stack, (((1,), (1,)), ((), ())),
                             preferred_element_type=F32)
    row = lax.broadcasted_iota(jnp.int32, (2 * c, c), 0) % c
    col = lax.broadcasted_iota(jnp.int32, (2 * c, c), 1)
    keep = (col >= row) if reverse else (col <= row)
    p = jnp.where(keep, scores, 0.0).astype(BF16)

    qi = per_head(q_inter)
    st = state_t.astype(BF16)
    dn_nt = (((1,), (1,)), ((), ()))
    o_a = (jnp.dot(p[:c], v[:, :GLA_DV], preferred_element_type=F32)
           + lax.dot_general(qi[:c], st[:GLA_DV], dn_nt, preferred_element_type=F32))
    o_b = (jnp.dot(p[c:], v[:, GLA_DV:], preferred_element_type=F32)
           + lax.dot_general(qi[c:], st[GLA_DV:], dn_nt, preferred_element_type=F32))
    new_state = jnp.exp(b_last) * state_t + jnp.dot(v.T, k_state, preferred_element_type=F32)
    return jnp.concatenate([o_a, o_b], axis=1), new_state


def _gla_kernel(*refs, reverse, n_chunks, write_o, final):
    q_ref, k_ref, v_ref, b_ref, s0_ref = refs[:5]
    pos = 5
    if final:
        of_ref, r_ref, nw_ref = refs[pos:pos + 3]
        pos += 3
    if write_o:
        o_ref = refs[pos]
        pos += 1
    sout_ref, state_ref = refs[pos], refs[pos + 1]
    j = pl.program_id(2)
    c = GLA_CHUNK

    @pl.when(j == 0)
    def _():
        state_ref[...] = s0_ref[...]

    def body(step, carry):
        ci = (n_chunks - 1 - step) if reverse else step
        r0 = pl.multiple_of(ci * c, c)
        q = q_ref[pl.ds(r0, c), :].astype(F32) * (GLA_DK ** -0.5)
        k = k_ref[pl.ds(r0, c), :].astype(F32)
        v = v_ref[pl.ds(r0, c), :]
        b = b_ref[pl.ds(r0, c), :]
        o, new_state = _gla_chunk(q, k, v, b, state_ref[...], reverse=reverse)
        state_ref[...] = new_state
        if final:
            o = o + of_ref[pl.ds(r0, c), :]
            gate = r_ref[pl.ds(r0, c), :].astype(F32)
            gate = gate * _sigmoid(gate)
            nw = nw_ref[...]
            ys = []
            for h in range(2):
                oh = o[:, h * GLA_DV:(h + 1) * GLA_DV]
                ms = jnp.mean(oh * oh, axis=-1, keepdims=True)
                ys.append(oh * lax.rsqrt(ms + EPS) * nw)
            o_ref[pl.ds(r0, c), :] = (jnp.concatenate(ys, axis=1) * gate).astype(o_ref.dtype)
        elif write_o:
            o_ref[pl.ds(r0, c), :] = o
        return carry

    lax.fori_loop(0, n_chunks, body, 0)

    @pl.when(j == pl.num_programs(2) - 1)
    def _():
        sout_ref[...] = state_ref[...]


def _gla(qkv, bcum, s0, *, row0, seg_len, n_batch, tb, reverse, write_o, o_rows=0, final=None):
    n_pairs = GLA_HEADS // 2
    nblk = seg_len // tb
    assert row0 % tb == 0 and seg_len % tb == 0 and tb % GLA_CHUNK == 0
    qw, vw = LANES, 2 * GLA_DV
    k_col0 = GLA_QK // qw
    v_col0 = (2 * GLA_QK) // vw
    r_col0 = (2 * GLA_QK + GLA_V) // vw
    b_col0 = (GLA_QK // qw) if reverse else 0

    def blk(b, j):
        jj = (nblk - 1 - j) if reverse else j
        return row0 // tb + b * nblk + jj

    def oblk(b, j):
        jj = (nblk - 1 - j) if reverse else j
        return b * nblk + jj

    in_specs = [pl.BlockSpec((tb, qw), lambda b, p, j: (blk(b, j), p)),
                pl.BlockSpec((tb, qw), lambda b, p, j: (blk(b, j), k_col0 + p)),
                pl.BlockSpec((tb, vw), lambda b, p, j: (blk(b, j), v_col0 + p)),
                pl.BlockSpec((tb, qw), lambda b, p, j: (blk(b, j), b_col0 + p)),
                pl.BlockSpec((None, None, vw, qw), lambda b, p, j: (b, p, 0, 0))]
    args = [qkv, qkv, qkv, bcum, s0]
    if final is not None:
        o_fwd, onw = final
        in_specs += [pl.BlockSpec((tb, vw), lambda b, p, j: (oblk(b, j), p)),
                     pl.BlockSpec((tb, vw), lambda b, p, j: (blk(b, j), r_col0 + p)),
                     pl.BlockSpec((1, GLA_DV), lambda b, p, j: (0, 0))]
        args += [o_fwd, qkv, onw.reshape(1, GLA_DV)]
    out_specs, out_shape = [], []
    if write_o:
        out_specs.append(pl.BlockSpec((tb, vw), lambda b, p, j: (oblk(b, j), p)))
        out_shape.append(jax.ShapeDtypeStruct((o_rows, GLA_V), BF16 if final is not None else F32))
    out_specs.append(pl.BlockSpec((None, None, vw, qw), lambda b, p, j: (b, p, 0, 0)))
    out_shape.append(jax.ShapeDtypeStruct(s0.shape, F32))
    outs = pl.pallas_call(
        functools.partial(_gla_kernel, reverse=reverse, n_chunks=tb // GLA_CHUNK,
                          write_o=write_o, final=final is not None),
        grid=(n_batch, n_pairs, nblk),
        in_specs=in_specs,
        out_specs=out_specs,
        out_shape=out_shape,
        scratch_shapes=[pltpu.VMEM((vw, qw), F32)],
        compiler_params=_params(3),
        name="gla_" + ("bwd" if reverse else "fwd") + ("_o" if write_o else "_state"),
    )(*args)
    return outs


def _qk_prep_kernel(x_ref, w_ref, g_ref, cos_ref, sa_ref, sb_ref, o_ref, *, scale, n_heads):
    g = g_ref[...]
    w = w_ref[...]
    cos, sa, sb = cos_ref[...], sa_ref[...], sb_ref[...]
    for h in range(n_heads):
        sl = slice(h * LANES, (h + 1) * LANES)
        x = x_ref[:, sl].astype(F32)
        hi, lo = _split_bf16(x * x)
        ms = (jnp.dot(hi, g, preferred_element_type=F32) + jnp.dot(lo, g, preferred_element_type=F32))
        y = x * lax.rsqrt(ms + EPS) * w
        half = DIFF_DQK // 4
        y = y * cos + pltpu.roll(y, LANES - half, axis=1) * sa + pltpu.roll(y, half, axis=1) * sb
        o_ref[:, sl] = (y * scale).astype(o_ref.dtype)


def _qk_prep(qkv, w, tables, *, col0, rows, seq, scale, tm=512):
    n = DIFF_QK
    cos_t, sa_t, sb_t = tables
    per_b = seq // tm
    n_lat = (rows // tm) if rows % seq == 0 else (rows // tm - 1)
    lane = np.arange(LANES)
    gmat = jnp.asarray((lane[:, None] // DIFF_DQK == lane[None, :] // DIFF_DQK) / DIFF_DQK, BF16)
    w2 = jnp.tile(w.reshape(1, DIFF_DQK), (1, LANES // DIFF_DQK))

    def trow(i):
        return jnp.where(i < n_lat, i % per_b, per_b)

    tspec = pl.BlockSpec((tm, LANES), lambda i: (trow(i), 0))
    return pl.pallas_call(
        functools.partial(_qk_prep_kernel, scale=scale, n_heads=DIFF_HEADS),
        grid=(rows // tm,),
        in_specs=[pl.BlockSpec((tm, n), lambda i: (i, col0 // n)),
                  pl.BlockSpec((1, LANES), lambda i: (0, 0)),
                  pl.BlockSpec((LANES, LANES), lambda i: (0, 0)),
                  tspec, tspec, tspec],
        out_specs=pl.BlockSpec((tm, n), lambda i: (i, 0)),
        out_shape=jax.ShapeDtypeStruct((rows, n), BF16),
        compiler_params=_params(1),
        name="qk_prep",
    )(qkv, w2, gmat, cos_t, sa_t, sb_t)


def _rope_tables(seq, ctx_rows):
    n_freq = DIFF_DQK // 4
    t = jnp.arange(seq)
    inv_freq = ROPE_THETA ** (-jnp.arange(n_freq, dtype=F32) / n_freq)
    ang_row = (t // GRID_W).astype(F32)[:, None] * inv_freq
    ang_col = (t % GRID_W).astype(F32)[:, None] * inv_freq
    lane = np.arange(LANES)
    use_col = (lane % DIFF_DQK) >= DIFF_DQK // 2
    first_half = (lane % (2 * n_freq)) < n_freq
    ang = jnp.where(use_col[None, :], ang_col[:, lane % n_freq], ang_row[:, lane % n_freq])
    cos, sin = jnp.cos(ang), jnp.sin(ang)
    sa = jnp.where(first_half[None, :], -sin, 0.0)
    sb = jnp.where(first_half[None, :], 0.0, sin)
    pad = lambda a, v: jnp.concatenate([a, jnp.full((ctx_rows, LANES), v, F32)], axis=0)
    return pad(cos, 1.0), pad(sa, 0.0), pad(sb, 0.0)


def _diff_attn_kernel(q_ref, kc_ref, kl_ref, vc_ref, vl_ref, lam_ref, w_ref, o_ref, acc_ref, *, tq, tk):
    q = q_ref[...]
    lane = lax.broadcasted_iota(jnp.int32, q.shape, 1)
    zero = jnp.zeros_like(q)
    first = lane < DIFF_DQK
    qs = jnp.concatenate([jnp.where(first, q, zero), jnp.where(first, zero, q)], axis=0)

    def step(kblk, vblk, m, l):
        s_t = lax.dot_general(kblk, qs, (((1,), (1,)), ((), ())), preferred_element_type=F32)
        m_new = jnp.maximum(m, jnp.max(s_t, axis=0, keepdims=True))
        alpha = jnp.exp2(m - m_new)
        p = jnp.exp2(s_t - m_new)
        l_new = alpha * l + jnp.sum(p, axis=0, keepdims=True)
        acc_ref[...] = alpha * acc_ref[...] + jnp.dot(vblk, p.astype(BF16), preferred_element_type=F32)
        return m_new, l_new

    acc_ref[...] = jnp.zeros_like(acc_ref)
    m0 = jnp.full((1, 2 * tq), -1e30, F32)
    l0 = jnp.zeros((1, 2 * tq), F32)
    m, l = step(kc_ref[...], vc_ref[...], m0, l0)

    def body(i, carry):
        r = pl.multiple_of(i * tk, tk)
        return step(kl_ref[pl.ds(r, tk), :], vl_ref[i], *carry)

    m, l = lax.fori_loop(0, kl_ref.shape[0] // tk, body, (m, l))

    lam_v = lam_ref[...]
    lam = (jnp.exp(jnp.sum(lam_v[0:1] * lam_v[1:2], keepdims=True))
           - jnp.exp(jnp.sum(lam_v[2:3] * lam_v[3:4], keepdims=True)) + LAM_INIT)
    acc = acc_ref[...]
    o_t = acc[:, :tq] / l[:, :tq] - lam * (acc[:, tq:] / l[:, tq:])
    ms = jnp.mean(o_t * o_t, axis=0, keepdims=True)
    y = o_t * lax.rsqrt(ms + EPS) * w_ref[...] * (1.0 - LAM_INIT)
    o_ref[...] = y.T.astype(o_ref.dtype)


def _diff_attn(qh, kh, vt, lam_vecs, onorm_w, *, n_batch, seq, ctx_len, tq=512, tk=256):
    assert ctx_len == tk and seq % tq == 0
    nq = seq // tq
    lat_blocks = seq // tk
    return pl.pallas_call(
        functools.partial(_diff_attn_kernel, tq=tq, tk=tk),
        grid=(n_batch, DIFF_HEADS, nq),
        in_specs=[pl.BlockSpec((tq, LANES), lambda b, h, i: (b * nq + i, h)),
                  pl.BlockSpec((ctx_len, LANES), lambda b, h, i: (n_batch * seq // ctx_len + b, h)),
                  pl.BlockSpec((seq, LANES), lambda b, h, i: (b, h)),
                  pl.BlockSpec((None, DIFF_DV, tk), lambda b, h, i: (n_batch * lat_blocks + b, h, 0)),
                  pl.BlockSpec((lat_blocks, DIFF_DV, tk), lambda b, h, i: (b, h, 0)),
                  pl.BlockSpec((4, DIFF_DQK), lambda b, h, i: (0, 0)),
                  pl.BlockSpec((DIFF_DV, 1), lambda b, h, i: (0, 0))],
        out_specs=pl.BlockSpec((tq, DIFF_DV), lambda b, h, i: (b * nq + i, h)),
        out_shape=jax.ShapeDtypeStruct((n_batch * seq, DIFF_V), BF16),
        scratch_shapes=[pltpu.VMEM((DIFF_DV, 2 * tq), F32)],
        compiler_params=_params(3),
        name="diff_attn",
    )(qh, kh, kh, vt, vt, lam_vecs, onorm_w.reshape(DIFF_DV, 1))


def _conv_gate_kernel(ug_ref, uv_ref, pg_ref, pv_ref, ng_ref, nv_ref, wg_ref, wv_ref, bg_ref, bv_ref,
                      o_ref, *, tm, seq, halo):
    t0 = (pl.program_id(0) * tm) % seq
    first = t0 == 0
    last = t0 + tm == seq
    rid = lax.broadcasted_iota(jnp.int32, ug_ref.shape, 0)

    def conv(u_ref, p_ref, n_ref, w_ref, b_ref):
        u = u_ref[...].astype(F32)
        prow = jnp.where(first, 0.0, p_ref[halo - 1:halo, :].astype(F32))
        nrow = jnp.where(last, 0.0, n_ref[0:1, :].astype(F32))
        up = jnp.where(rid == 0, prow, pltpu.roll(u, 1, axis=0))
        un = jnp.where(rid == tm - 1, nrow, pltpu.roll(u, tm - 1, axis=0))
        w = w_ref[...]
        return up * w[0:1] + u * w[1:2] + un * w[2:3] + b_ref[...]

    cg = conv(ug_ref, pg_ref, ng_ref, wg_ref, bg_ref)
    cv = conv(uv_ref, pv_ref, nv_ref, wv_ref, bv_ref)
    o_ref[...] = (cg * _sigmoid(cg) * cv).astype(o_ref.dtype)


def _conv_gate(u, conv_w, conv_b, *, seq, tm=512, tn=512, halo=16):
    rows, n2 = u.shape
    dff = n2 // 2
    nj = dff // tn
    hb = tm // halo
    nhalo = rows // halo
    main = lambda off: pl.BlockSpec((tm, tn), lambda i, j: (i, off + j))
    prev = lambda off: pl.BlockSpec((halo, tn), lambda i, j: (jnp.maximum(i * hb - 1, 0), off + j))
    nxt = lambda off: pl.BlockSpec((halo, tn), lambda i, j: (jnp.minimum((i + 1) * hb, nhalo - 1), off + j))
    wsp = lambda off: pl.BlockSpec((CONV_W, tn), lambda i, j: (0, off + j))
    bsp = lambda off: pl.BlockSpec((1, tn), lambda i, j: (0, off + j))
    cb = conv_b.reshape(1, n2)
    return pl.pallas_call(
        functools.partial(_conv_gate_kernel, tm=tm, seq=seq, halo=halo),
        grid=(rows // tm, nj),
        in_specs=[main(0), main(nj), prev(0), prev(nj), nxt(0), nxt(nj), wsp(0), wsp(nj), bsp(0), bsp(nj)],
        out_specs=pl.BlockSpec((tm, tn), lambda i, j: (i, j)),
        out_shape=jax.ShapeDtypeStruct((rows, dff), BF16),
        compiler_params=_params(2),
        name="conv_gate",
    )(u, u, u, u, u, u, conv_w, conv_w, cb, cb)


def kernel(x, c, ctx, c_ctx, w_ada, b_ada, norm1_w, w_in, w_a_up_f, b_a_f, w_a_up_b, b_a_b, gla_onorm_w, diff_qnorm_w, diff_knorm_w, lambda_q1, lambda_k1, lambda_q2, lambda_k2, diff_onorm_w, w_proj_gla, w_proj_diff, w_gate, b_gate, w_out, norm2_w, w_up, conv_w, conv_b, w_down):
    n_batch, seq, d = x.shape
    ctx_len = ctx.shape[1]
    assert w_ada.shape[0] == 1, "single-layer block"
    rows = n_batch * seq
    crow = n_batch * ctx_len

    cs = jnp.concatenate([c, c_ctx[None, :], jnp.zeros((8 - n_batch - 1, d), F32)], axis=0)
    mod = _ada(cs, w_ada[0], b_ada[0])
    mod3 = mod.reshape(8, 1, 6 * d)

    o_r = 2 * GLA_QK + 2 * GLA_V
    o_d = o_r + 2 * GLA_RANK
    o_v = o_d + 2 * DIFF_QK
    w_i = w_in[0]
    w_main = jnp.concatenate([w_i[:, :o_r], w_i[:, o_d:o_v]], axis=1).astype(BF16)
    w_dv_t = w_i[:, o_v:].T.astype(BF16)
    w_lr = jnp.pad(w_i[:, o_r:o_d], ((0, 0), (0, LANES - 2 * GLA_RANK))).astype(BF16)
    w_upcat = jnp.zeros((LANES, 2 * GLA_QK), F32)
    w_upcat = w_upcat.at[:GLA_RANK, :GLA_QK].set(w_a_up_f[0])
    w_upcat = w_upcat.at[GLA_RANK:2 * GLA_RANK, GLA_QK:].set(w_a_up_b[0]).astype(BF16)
    b_upcat = jnp.concatenate([b_a_f[0], b_a_b[0]]).reshape(1, 2 * GLA_QK)

    x2 = x.reshape(rows, d)
    h_all = _norm1(x2, ctx.reshape(crow, d), norm1_w[0], mod3, seq=seq, n_batch=n_batch)
    all_rows = rows + crow
    qkv = _mm(h_all, w_main, rows=all_rows, tm=all_rows // 8, tn=1024, out_dtype=BF16, name="in_proj")
    v_t = _mm_vt(w_dv_t, h_all)
    bcum = _decay(h_all, w_lr, w_upcat, b_upcat)

    s_zero = jnp.zeros((n_batch, GLA_HEADS // 2, 2 * GLA_DV, LANES), F32)
    seg_c = dict(row0=rows, seg_len=ctx_len, n_batch=n_batch, tb=ctx_len)
    seg_l = dict(row0=0, seg_len=seq, n_batch=n_batch, tb=1024)
    (s_cf,) = _gla(qkv, bcum, s_zero, reverse=False, write_o=False, **seg_c)
    (s_cb,) = _gla(qkv, bcum, s_zero, reverse=True, write_o=False, **seg_c)
    o_lf, _ = _gla(qkv, bcum, s_cf, reverse=False, write_o=True, o_rows=rows, **seg_l)
    y_a, _ = _gla(qkv, bcum, s_cb, reverse=True, write_o=True, o_rows=rows,
                  final=(o_lf, gla_onorm_w[0]), **seg_l)

    tables = _rope_tables(seq, 512)
    q_col0 = o_r
    k_col0 = o_r + DIFF_QK
    qh = _qk_prep(qkv, diff_qnorm_w[0], tables, col0=q_col0, rows=rows, seq=seq,
                  scale=(DIFF_DQK ** -0.5) * LOG2E)
    kh = _qk_prep(qkv, diff_knorm_w[0], tables, col0=k_col0, rows=all_rows, seq=seq, scale=1.0)
    lam_vecs = jnp.stack([lambda_q1[0], lambda_k1[0], lambda_q2[0], lambda_k2[0]], axis=0)
    y_b = _diff_attn(qh, kh, v_t, lam_vecs, diff_onorm_w[0], n_batch=n_batch, seq=seq, ctx_len=ctx_len)

    g = _mm_gate(h_all, w_gate[0].astype(BF16), b_gate[0], rows=rows)
    merged = _merge(y_a, y_b, w_proj_gla[0].astype(BF16), w_proj_diff[0].astype(BF16), g)
    x1 = _resid_mm(merged, w_out[0].astype(BF16), x2, mod3, gate_chunk=2, seq=seq,
                   tm=1024, tn=1024, tk=d, name="out_proj")

    h2 = _norm2(x1, norm2_w[0], mod3, seq=seq)
    u = _mm(h2, w_up[0].astype(BF16), rows=rows, tm=1024, tn=1024, out_dtype=BF16, name="ffn_up")
    act = _conv_gate(u, conv_w[0], conv_b[0], seq=seq)
    dff = act.shape[1]
    out = _resid_mm(act, w_down[0].astype(BF16), x1, mod3, gate_chunk=5, seq=seq,
                    tm=512, tn=1024, tk=dff // 2, name="ffn_down")
    return out.reshape(n_batch, seq, d)
```

```python
import functools
import math

import jax
import jax.numpy as jnp
import numpy as np
from jax import lax
from jax.experimental import pallas as pl
from jax.experimental.pallas import tpu as pltpu

F32 = jnp.float32
BF16 = jnp.bfloat16

GRID_W = 64
GLA_HEADS = 8
GLA_DK = 64
GLA_DV = 128
GLA_RANK = 16
GLA_TAU = 16.0
DIFF_HEADS = 8
DIFF_DQK = 64
DIFF_DV = 128
CONV_W = 3
ROPE_THETA = 10000.0
EPS = 1e-6
LAM_INIT = 0.8 - 0.6 * math.exp(-0.3 * 0)

GLA_QK = GLA_HEADS * GLA_DK
GLA_V = GLA_HEADS * GLA_DV
DIFF_QK = DIFF_HEADS * 2 * DIFF_DQK
DIFF_V = DIFF_HEADS * DIFF_DV

LANES = 128
VMEM_LIMIT = 56 * 1024 * 1024

GLA_CHUNK = 64
GLA_REFS = 4
GLA_UNROLL = 8
LOG2E = 1.4426950408889634


def _params(n_axes):
    return pltpu.CompilerParams(dimension_semantics=("arbitrary",) * n_axes,
                                vmem_limit_bytes=VMEM_LIMIT)


def _sigmoid(x):
    return 1.0 / (1.0 + jnp.exp(-x))


def _split_bf16(x):
    hi = x.astype(BF16)
    lo = (x - hi.astype(F32)).astype(BF16)
    return hi, lo


def _ada_kernel(c_ref, w_ref, b_ref, o_ref):
    c = c_ref[...]
    s = (c * _sigmoid(c)).astype(BF16)
    o_ref[...] = jnp.dot(s, w_ref[...].astype(BF16), preferred_element_type=F32) + b_ref[...]


def _ada(cs, w, b):
    d, n = w.shape
    tn = 1024
    return pl.pallas_call(
        _ada_kernel,
        grid=(n // tn,),
        in_specs=[pl.BlockSpec((8, d), lambda j: (0, 0)),
                  pl.BlockSpec((d, tn), lambda j: (0, j)),
                  pl.BlockSpec((1, tn), lambda j: (0, j))],
        out_specs=pl.BlockSpec((8, tn), lambda j: (0, j)),
        out_shape=jax.ShapeDtypeStruct((8, n), F32),
        compiler_params=_params(1),
        name="ada",
    )(cs, w, b.reshape(1, n))


def _norm_mod(x, nw, sh, sc):
    ms = jnp.mean(x * x, axis=-1, keepdims=True)
    y = x * lax.rsqrt(ms + EPS) * nw
    return (y * (1.0 + sc) + sh).astype(BF16)


def _norm1_kernel(x_ref, ctx_ref, nw_ref, sh_ref, sc_ref, o_ref, *, n_lat_tiles):
    i = pl.program_id(0)

    @pl.when(i < n_lat_tiles)
    def _():
        o_ref[...] = _norm_mod(x_ref[...], nw_ref[...], sh_ref[...], sc_ref[...])

    @pl.when(i >= n_lat_tiles)
    def _():
        o_ref[...] = _norm_mod(ctx_ref[...], nw_ref[...], sh_ref[...], sc_ref[...])


def _norm1(x2, ctx2, nw, mod3, *, seq, n_batch, tm=512):
    rows, d = x2.shape
    crow = ctx2.shape[0]
    assert crow == tm and rows % tm == 0 and seq % tm == 0
    nl = rows // tm
    per_b = seq // tm

    def mod_row(i):
        return jnp.where(i < nl, i // per_b, n_batch)

    return pl.pallas_call(
        functools.partial(_norm1_kernel, n_lat_tiles=nl),
        grid=(nl + 1,),
        in_specs=[pl.BlockSpec((tm, d), lambda i: (jnp.minimum(i, nl - 1), 0)),
                  pl.BlockSpec((tm, d), lambda i: (0, 0)),
                  pl.BlockSpec((1, d), lambda i: (0, 0)),
                  pl.BlockSpec((None, 1, d), lambda i: (mod_row(i), 0, 0)),
                  pl.BlockSpec((None, 1, d), lambda i: (mod_row(i), 0, 1))],
        out_specs=pl.BlockSpec((tm, d), lambda i: (i, 0)),
        out_shape=jax.ShapeDtypeStruct((rows + crow, d), BF16),
        compiler_params=_params(1),
        name="norm1",
    )(x2, ctx2, nw.reshape(1, d), mod3, mod3)


def _norm2_kernel(x_ref, nw_ref, sh_ref, sc_ref, o_ref):
    o_ref[...] = _norm_mod(x_ref[...], nw_ref[...], sh_ref[...], sc_ref[...])


def _norm2(x2, nw, mod3, *, seq, tm=512):
    rows, d = x2.shape
    per_b = seq // tm
    return pl.pallas_call(
        _norm2_kernel,
        grid=(rows // tm,),
        in_specs=[pl.BlockSpec((tm, d), lambda i: (i, 0)),
                  pl.BlockSpec((1, d), lambda i: (0, 0)),
                  pl.BlockSpec((None, 1, d), lambda i: (i // per_b, 0, 3)),
                  pl.BlockSpec((None, 1, d), lambda i: (i // per_b, 0, 4))],
        out_specs=pl.BlockSpec((tm, d), lambda i: (i, 0)),
        out_shape=jax.ShapeDtypeStruct((rows, d), BF16),
        compiler_params=_params(1),
        name="norm2",
    )(x2, nw.reshape(1, d), mod3, mod3)


def _mm_kernel(x_ref, w_ref, o_ref):
    o_ref[...] = jnp.dot(x_ref[...], w_ref[...], preferred_element_type=F32).astype(o_ref.dtype)


def _mm(x, w, *, rows, tm, tn, out_dtype, name):
    k, n = w.shape
    return pl.pallas_call(
        _mm_kernel,
        grid=(n // tn, rows // tm),
        in_specs=[pl.BlockSpec((tm, k), lambda j, i: (i, 0)),
                  pl.BlockSpec((k, tn), lambda j, i: (0, j))],
        out_specs=pl.BlockSpec((tm, tn), lambda j, i: (i, j)),
        out_shape=jax.ShapeDtypeStruct((rows, n), out_dtype),
        compiler_params=_params(2),
        name=name,
    )(x, w)


def _mm_gate_kernel(x_ref, w_ref, b_ref, o_ref):
    z = jnp.dot(x_ref[...], w_ref[...], preferred_element_type=F32) + b_ref[...]
    o_ref[...] = _sigmoid(z).astype(o_ref.dtype)


def _mm_gate(x, w, b, *, rows, tm=1024, tn=1024):
    k, n = w.shape
    return pl.pallas_call(
        _mm_gate_kernel,
        grid=(n // tn, rows // tm),
        in_specs=[pl.BlockSpec((tm, k), lambda j, i: (i, 0)),
                  pl.BlockSpec((k, tn), lambda j, i: (0, j)),
                  pl.BlockSpec((1, tn), lambda j, i: (0, j))],
        out_specs=pl.BlockSpec((tm, tn), lambda j, i: (i, j)),
        out_shape=jax.ShapeDtypeStruct((rows, n), BF16),
        compiler_params=_params(2),
        name="gate",
    )(x, w, b.reshape(1, n))


ONES_ROWS = 16


def _vt_kernel(w_ref, x_ref, o_ref):
    vt = lax.dot_general(w_ref[...], x_ref[...], (((1,), (1,)), ((), ())),
                         preferred_element_type=F32).astype(o_ref.dtype)
    n_heads, _, tn = o_ref.shape
    for h in range(n_heads):
        o_ref[h, :DIFF_DV, :] = vt[h * DIFF_DV:(h + 1) * DIFF_DV]
        o_ref[h, DIFF_DV:, :] = jnp.ones((ONES_ROWS, tn), o_ref.dtype)


def _kv_block_order(j, *, n_batch, seq, tn):
    lat = seq // tn
    n_lat = n_batch * lat
    return jnp.where(j < n_lat, (j // lat) * (lat + 1) + 1 + j % lat, (j - n_lat) * (lat + 1))


def _mm_vt(wt, x, *, n_batch, seq, tn=256):
    n_out, k = wt.shape
    rows = x.shape[0]
    order = functools.partial(_kv_block_order, n_batch=n_batch, seq=seq, tn=tn)
    blk = (DIFF_HEADS, DIFF_DV + ONES_ROWS, tn)
    return pl.pallas_call(
        _vt_kernel,
        grid=(rows // tn,),
        in_specs=[pl.BlockSpec((n_out, k), lambda j: (0, 0)),
                  pl.BlockSpec((tn, k), lambda j: (j, 0))],
        out_specs=pl.BlockSpec((None,) + blk, lambda j: (order(j), 0, 0, 0)),
        out_shape=jax.ShapeDtypeStruct((rows // tn,) + blk, BF16),
        compiler_params=_params(1),
        name="v_transposed",
    )(wt, x)


def _merge_kernel(ya_ref, yb_ref, wa_ref, wb_ref, ga_ref, gb_ref, o_ref):
    pa = jnp.dot(ya_ref[...], wa_ref[...], preferred_element_type=F32)
    pb = jnp.dot(yb_ref[...], wb_ref[...], preferred_element_type=F32)
    o_ref[...] = (ga_ref[...].astype(F32) * pa + gb_ref[...].astype(F32) * pb).astype(o_ref.dtype)


def _merge(ya, yb, wa, wb, g, *, tm=1024, tn=1024):
    rows, ka = ya.shape
    kb = yb.shape[1]
    n = wa.shape[1]
    nb = n // tn
    return pl.pallas_call(
        _merge_kernel,
        grid=(nb, rows // tm),
        in_specs=[pl.BlockSpec((tm, ka), lambda j, i: (i, 0)),
                  pl.BlockSpec((tm, kb), lambda j, i: (i, 0)),
                  pl.BlockSpec((ka, tn), lambda j, i: (0, j)),
                  pl.BlockSpec((kb, tn), lambda j, i: (0, j)),
                  pl.BlockSpec((tm, tn), lambda j, i: (i, j)),
                  pl.BlockSpec((tm, tn), lambda j, i: (i, nb + j))],
        out_specs=pl.BlockSpec((tm, tn), lambda j, i: (i, j)),
        out_shape=jax.ShapeDtypeStruct((rows, n), BF16),
        compiler_params=_params(2),
        name="merge",
    )(ya, yb, wa, wb, g, g)


def _resid_kernel(a_ref, w_ref, x_ref, g_ref, o_ref, acc_ref, *, nk):
    kk = pl.program_id(2)
    part = jnp.dot(a_ref[...], w_ref[...], preferred_element_type=F32)

    if nk == 1:
        o_ref[...] = x_ref[...] + g_ref[...] * part
    else:
        @pl.when(kk == 0)
        def _():
            acc_ref[...] = part

        @pl.when(jnp.logical_and(kk > 0, kk < nk - 1))
        def _():
            acc_ref[...] += part

        @pl.when(kk == nk - 1)
        def _():
            o_ref[...] = x_ref[...] + g_ref[...] * (acc_ref[...] + part)


def _resid_mm(a, w, xres, mod3, *, gate_chunk, seq, tm, tn, tk, name):
    rows, k = a.shape
    n = w.shape[1]
    nk = k // tk
    per_b = seq // tm
    cpc = n // tn
    return pl.pallas_call(
        functools.partial(_resid_kernel, nk=nk),
        grid=(n // tn, rows // tm, nk),
        in_specs=[pl.BlockSpec((tm, tk), lambda j, i, kk: (i, kk)),
                  pl.BlockSpec((tk, tn), lambda j, i, kk: (kk, j)),
                  pl.BlockSpec((tm, tn), lambda j, i, kk: (i, j)),
                  pl.BlockSpec((None, 1, tn), lambda j, i, kk: (i // per_b, 0, gate_chunk * cpc + j))],
        out_specs=pl.BlockSpec((tm, tn), lambda j, i, kk: (i, j)),
        out_shape=jax.ShapeDtypeStruct((rows, n), F32),
        scratch_shapes=[pltpu.VMEM((tm, tn), F32)],
        compiler_params=_params(3),
        name=name,
    )(a, w, xres, mod3)


def _decay_kernel(h_ref, wa_ref, wup_ref, bup_ref, tri_f_ref, tri_b_ref, o_ref, *, half):
    ga = jnp.dot(h_ref[...], wa_ref[...], preferred_element_type=F32)
    z = jnp.dot(ga.astype(BF16), wup_ref[...], preferred_element_type=F32) + bup_ref[...]
    la = (jnp.minimum(z, 0.0) - jnp.log1p(jnp.exp(-jnp.abs(z)))) * (1.0 / GLA_TAU)
    hi, lo = _split_bf16(la)
    for tri_ref, sl in ((tri_f_ref, slice(0, half)), (tri_b_ref, slice(half, 2 * half))):
        tri = tri_ref[...]
        o_ref[:, sl] = (jnp.dot(tri, hi[:, sl], preferred_element_type=F32)
                        + jnp.dot(tri, lo[:, sl], preferred_element_type=F32))


def _decay(h, wa, wup, bup, *, tm=512):
    rows, k = h.shape
    n = wup.shape[1]
    r = np.arange(tm)
    same = (r[:, None] // GLA_CHUNK) == (r[None, :] // GLA_CHUNK)
    tri_f = jnp.asarray(same & (r[None, :] <= r[:, None]), BF16)
    tri_b = jnp.asarray(same & (r[None, :] >= r[:, None]), BF16)
    return pl.pallas_call(
        functools.partial(_decay_kernel, half=n // 2),
        grid=(rows // tm,),
        in_specs=[pl.BlockSpec((tm, k), lambda i: (i, 0)),
                  pl.BlockSpec((k, LANES), lambda i: (0, 0)),
                  pl.BlockSpec((LANES, n), lambda i: (0, 0)),
                  pl.BlockSpec((1, n), lambda i: (0, 0)),
                  pl.BlockSpec((tm, tm), lambda i: (0, 0)),
                  pl.BlockSpec((tm, tm), lambda i: (0, 0))],
        out_specs=pl.BlockSpec((tm, n), lambda i: (i, 0)),
        out_shape=jax.ShapeDtypeStruct((rows, n), F32),
        compiler_params=_params(1),
        name="decay",
    )(h, wa, wup, bup, tri_f, tri_b)


def _gla_operands(q, k, b, *, reverse):
    c = GLA_CHUNK
    bs = c // GLA_REFS
    b_last = b[0:1, :] if reverse else b[c - 1:c, :]

    q_inter = q * jnp.exp(b)
    k_state = (k * jnp.exp(b_last - b)).astype(BF16)

    def rows_padded(t, lo):
        parts = [jnp.zeros((n, LANES), F32) for n in (lo,) if n] + [t]
        parts += [jnp.zeros((n, LANES), F32) for n in (c - lo - t.shape[0],) if n]
        return parts[0] if len(parts) == 1 else jnp.concatenate(parts, axis=0)

    q_segs, k_segs = [], []
    for r in range(GLA_REFS):
        lo, hi = r * bs, (r + 1) * bs
        m_r = b[lo:lo + 1, :] if reverse else b[hi - 1:hi, :]
        k_segs.append(rows_padded(k[lo:hi] * jnp.exp(m_r - b[lo:hi]), lo))
        if reverse:
            q_segs.append(rows_padded(q[:hi] * jnp.exp(b[:hi] - m_r), 0))
        else:
            q_segs.append(rows_padded(q[lo:] * jnp.exp(b[lo:] - m_r), lo))

    def per_head(t):
        ln = lax.broadcasted_iota(jnp.int32, t.shape, 1)
        ma = (ln % LANES) < GLA_DK
        return jnp.concatenate([jnp.where(ma, t, 0.0), jnp.where(ma, 0.0, t)], axis=0).astype(BF16)

    q_stack = per_head(jnp.concatenate(q_segs, axis=1))
    k_stack = jnp.concatenate(k_segs, axis=1).astype(BF16)
    return q_stack, k_stack, per_head(q_inter), k_state, jnp.exp(b_last)


def _gla_group(chunks, state_t, *, reverse):
    c = GLA_CHUNK
    dn_nt = (((1,), (1,)), ((), ()))
    ops = [_gla_operands(q, k, b, reverse=reverse) for q, k, _, b in chunks]
    scores = [lax.dot_general(o[0], o[1], dn_nt, preferred_element_type=F32) for o in ops]
    updates = [jnp.dot(ch[2].T, o[3], preferred_element_type=F32) for ch, o in zip(chunks, ops)]
    row = lax.broadcasted_iota(jnp.int32, (2 * c, c), 0) % c
    col = lax.broadcasted_iota(jnp.int32, (2 * c, c), 1)
    keep = (col >= row) if reverse else (col <= row)
    intra = []
    for ch, s in zip(chunks, scores):
        p = jnp.where(keep, s, 0.0).astype(BF16)
        v = ch[2]
        intra.append((jnp.dot(p[:c], v[:, :GLA_DV], preferred_element_type=F32),
                      jnp.dot(p[c:], v[:, GLA_DV:], preferred_element_type=F32)))
    outs = []
    for o, (ia, ib), upd in zip(ops, intra, updates):
        st = state_t.astype(BF16)
        qi = o[2]
        o_a = ia + lax.dot_general(qi[:c], st[:GLA_DV], dn_nt, preferred_element_type=F32)
        o_b = ib + lax.dot_general(qi[c:], st[GLA_DV:], dn_nt, preferred_element_type=F32)
        outs.append(jnp.concatenate([o_a, o_b], axis=1))
        state_t = o[4] * state_t + upd
    return outs, state_t


def _gla_kernel(*refs, reverse, n_chunks, unroll, write_o, final):
    q_ref, k_ref, v_ref, b_ref, s0_ref = refs[:5]
    pos = 5
    if final:
        of_ref, r_ref, nw_ref = refs[pos:pos + 3]
        pos += 3
    if write_o:
        o_ref = refs[pos]
        pos += 1
    sout_ref, state_ref = refs[pos], refs[pos + 1]
    j = pl.program_id(2)
    c = GLA_CHUNK

    @pl.when(j == 0)
    def _():
        state_ref[...] = s0_ref[...]

    def body(trip, carry):
        starts, chunks = [], []
        for u in range(unroll):
            step = trip * unroll + u
            ci = (n_chunks - 1 - step) if reverse else step
            r0 = pl.multiple_of(ci * c, c)
            starts.append(r0)
            chunks.append((q_ref[pl.ds(r0, c), :].astype(F32) * (GLA_DK ** -0.5),
                           k_ref[pl.ds(r0, c), :].astype(F32),
                           v_ref[pl.ds(r0, c), :],
                           b_ref[pl.ds(r0, c), :]))
        outs, new_state = _gla_group(chunks, state_ref[...], reverse=reverse)
        state_ref[...] = new_state
        for r0, o in zip(starts, outs):
            emit(r0, o)
        return carry

    def emit(r0, o):
        if final:
            o = o + of_ref[pl.ds(r0, c), :]
            gate = r_ref[pl.ds(r0, c), :].astype(F32)
            gate = gate * _sigmoid(gate)
            nw = nw_ref[...]
            ys = []
            for h in range(2):
                oh = o[:, h * GLA_DV:(h + 1) * GLA_DV]
                ms = jnp.mean(oh * oh, axis=-1, keepdims=True)
                ys.append(oh * lax.rsqrt(ms + EPS) * nw)
            o_ref[pl.ds(r0, c), :] = (jnp.concatenate(ys, axis=1) * gate).astype(o_ref.dtype)
        elif write_o:
            o_ref[pl.ds(r0, c), :] = o

    lax.fori_loop(0, n_chunks // unroll, body, 0)

    @pl.when(j == pl.num_programs(2) - 1)
    def _():
        sout_ref[...] = state_ref[...]


def _gla(qkv, bcum, s0, *, row0, seg_len, n_batch, tb, reverse, write_o, o_rows=0, final=None):
    n_pairs = GLA_HEADS // 2
    nblk = seg_len // tb
    assert row0 % tb == 0 and seg_len % tb == 0 and tb % GLA_CHUNK == 0
    qw, vw = LANES, 2 * GLA_DV
    k_col0 = GLA_QK // qw
    v_col0 = (2 * GLA_QK) // vw
    r_col0 = (2 * GLA_QK + GLA_V) // vw
    b_col0 = (GLA_QK // qw) if reverse else 0

    def blk(b, j):
        jj = (nblk - 1 - j) if reverse else j
        return row0 // tb + b * nblk + jj

    def oblk(b, j):
        jj = (nblk - 1 - j) if reverse else j
        return b * nblk + jj

    in_specs = [pl.BlockSpec((tb, qw), lambda b, p, j: (blk(b, j), p)),
                pl.BlockSpec((tb, qw), lambda b, p, j: (blk(b, j), k_col0 + p)),
                pl.BlockSpec((tb, vw), lambda b, p, j: (blk(b, j), v_col0 + p)),
                pl.BlockSpec((tb, qw), lambda b, p, j: (blk(b, j), b_col0 + p)),
                pl.BlockSpec((None, None, vw, qw), lambda b, p, j: (b, p, 0, 0))]
    args = [qkv, qkv, qkv, bcum, s0]
    if final is not None:
        o_fwd, onw = final
        in_specs += [pl.BlockSpec((tb, vw), lambda b, p, j: (oblk(b, j), p)),
                     pl.BlockSpec((tb, vw), lambda b, p, j: (blk(b, j), r_col0 + p)),
                     pl.BlockSpec((1, GLA_DV), lambda b, p, j: (0, 0))]
        args += [o_fwd, qkv, onw.reshape(1, GLA_DV)]
    out_specs, out_shape = [], []
    if write_o:
        out_specs.append(pl.BlockSpec((tb, vw), lambda b, p, j: (oblk(b, j), p)))
        out_shape.append(jax.ShapeDtypeStruct((o_rows, GLA_V), BF16 if final is not None else F32))
    out_specs.append(pl.BlockSpec((None, None, vw, qw), lambda b, p, j: (b, p, 0, 0)))
    out_shape.append(jax.ShapeDtypeStruct(s0.shape, F32))
    outs = pl.pallas_call(
        functools.partial(_gla_kernel, reverse=reverse, n_chunks=tb // GLA_CHUNK,
                          unroll=min(GLA_UNROLL, tb // GLA_CHUNK), write_o=write_o, final=final is not None),
        grid=(n_batch, n_pairs, nblk),
        in_specs=in_specs,
        out_specs=out_specs,
        out_shape=out_shape,
        scratch_shapes=[pltpu.VMEM((vw, qw), F32)],
        compiler_params=_params(3),
        name="gla_" + ("bwd" if reverse else "fwd") + ("_o" if write_o else "_state"),
    )(*args)
    return outs


def _qk_prep_kernel(x_ref, w_ref, g_ref, cos_ref, sa_ref, sb_ref, o_ref, *, scale, n_heads):
    g = g_ref[...]
    w = w_ref[...]
    cos, sa, sb = cos_ref[...], sa_ref[...], sb_ref[...]
    for h in range(n_heads):
        sl = slice(h * LANES, (h + 1) * LANES)
        x = x_ref[:, sl].astype(F32)
        hi, lo = _split_bf16(x * x)
        ms = (jnp.dot(hi, g, preferred_element_type=F32) + jnp.dot(lo, g, preferred_element_type=F32))
        y = x * lax.rsqrt(ms + EPS) * w
        half = DIFF_DQK // 4
        y = y * cos + pltpu.roll(y, LANES - half, axis=1) * sa + pltpu.roll(y, half, axis=1) * sb
        o_ref[:, sl] = (y * scale).astype(o_ref.dtype)


def _qk_prep(qkv, w, tables, *, col0, rows, tm, scale, table_map, out_map, out_rows):
    n = DIFF_QK
    cos_t, sa_t, sb_t = tables
    lane = np.arange(LANES)
    gmat = jnp.asarray((lane[:, None] // DIFF_DQK == lane[None, :] // DIFF_DQK) / DIFF_DQK, BF16)
    w2 = jnp.tile(w.reshape(1, DIFF_DQK), (1, LANES // DIFF_DQK))
    tspec = pl.BlockSpec((tm, LANES), lambda i: (table_map(i), 0))
    return pl.pallas_call(
        functools.partial(_qk_prep_kernel, scale=scale, n_heads=DIFF_HEADS),
        grid=(rows // tm,),
        in_specs=[pl.BlockSpec((tm, n), lambda i: (i, col0 // n)),
                  pl.BlockSpec((1, LANES), lambda i: (0, 0)),
                  pl.BlockSpec((LANES, LANES), lambda i: (0, 0)),
                  tspec, tspec, tspec],
        out_specs=pl.BlockSpec((tm, n), lambda i: (out_map(i), 0)),
        out_shape=jax.ShapeDtypeStruct((out_rows, n), BF16),
        compiler_params=_params(1),
        name="qk_prep",
    )(qkv, w2, gmat, cos_t, sa_t, sb_t)


def _rope_tables(seq, ctx_rows):
    n_freq = DIFF_DQK // 4
    t = jnp.arange(seq)
    inv_freq = ROPE_THETA ** (-jnp.arange(n_freq, dtype=F32) / n_freq)
    ang_row = (t // GRID_W).astype(F32)[:, None] * inv_freq
    ang_col = (t % GRID_W).astype(F32)[:, None] * inv_freq
    lane = np.arange(LANES)
    use_col = (lane % DIFF_DQK) >= DIFF_DQK // 2
    first_half = (lane % (2 * n_freq)) < n_freq
    ang = jnp.where(use_col[None, :], ang_col[:, lane % n_freq], ang_row[:, lane % n_freq])
    cos, sin = jnp.cos(ang), jnp.sin(ang)
    sa = jnp.where(first_half[None, :], -sin, 0.0)
    sb = jnp.where(first_half[None, :], 0.0, sin)
    pad = lambda a, v: jnp.concatenate([a, jnp.full((ctx_rows, LANES), v, F32)], axis=0)
    return pad(cos, 1.0), pad(sa, 0.0), pad(sb, 0.0)


def _diff_attn_kernel(q_ref, k_ref, v_ref, lam_ref, w_ref, o_ref,
                      qs_ref, s_ref, p_ref, acc_ref, m_ref, alpha_ref, bmax_ref, *, tq, tk, n_blk, group,
                      unroll):
    q = q_ref[...]
    lane = lax.broadcasted_iota(jnp.int32, q.shape, 1)
    zero = jnp.zeros_like(q)
    first = lane < DIFF_DQK
    qs_ref[...] = jnp.concatenate([jnp.where(first, q, zero), jnp.where(first, zero, q)], axis=0)
    width = 2 * tq

    def scores(i, slot):
        r = pl.multiple_of(i * tk, tk)
        kblk = k_ref[pl.ds(r, tk), :]
        for g0 in range(0, width, group):
            cols = slice(g0, g0 + group)
            s = lax.dot_general(kblk, qs_ref[cols, :], (((1,), (1,)), ((), ())),
                                preferred_element_type=F32).astype(BF16)
            s_ref[slot, :, cols] = s
            bmax_ref[slot, :, cols] = jnp.max(s, axis=0, keepdims=True).astype(F32)

    def softmax(slot):
        for g0 in range(0, width, group):
            cols = slice(g0, g0 + group)
            m_old = m_ref[:, cols]
            m_new = jnp.maximum(m_old, bmax_ref[slot, :, cols])
            m_ref[:, cols] = m_new
            alpha_ref[:, cols] = jnp.exp2(m_old - m_new)
            p_ref[slot, :, cols] = jnp.exp2(s_ref[slot, :, cols] - m_new.astype(BF16))

    def values(i, slot):
        acc_ref[...] = (alpha_ref[...] * acc_ref[...]
                        + jnp.dot(v_ref[i], p_ref[slot], preferred_element_type=F32))

    acc_ref[...] = jnp.zeros_like(acc_ref)
    m_ref[...] = jnp.full(m_ref.shape, -1e30, F32)
    scores(0, 0)
    scores(1, 1)
    softmax(0)

    def body(j, carry):
        for u in range(unroll):
            blk = 1 + unroll * j + u
            slot = (1 + u) % 2
            scores(jnp.minimum(blk + 1, n_blk - 1), 1 - slot)
            values(blk - 1, 1 - slot)
            softmax(slot)
        return carry

    lax.fori_loop(0, (n_blk - 1) // unroll, body, 0)
    values(n_blk - 1, 0)

    lam_v = lam_ref[...]
    lam = (jnp.exp(jnp.sum(lam_v[0:1] * lam_v[1:2], keepdims=True))
           - jnp.exp(jnp.sum(lam_v[2:3] * lam_v[3:4], keepdims=True)) + LAM_INIT)
    acc = acc_ref[:DIFF_DV, :]
    l = acc_ref[DIFF_DV:DIFF_DV + 1, :]
    o_t = acc[:, :tq] / l[:, :tq] - lam * (acc[:, tq:] / l[:, tq:])
    ms = jnp.mean(o_t * o_t, axis=0, keepdims=True)
    y = o_t * lax.rsqrt(ms + EPS) * w_ref[...] * (1.0 - LAM_INIT)
    o_ref[...] = y.T.astype(o_ref.dtype)


def _diff_attn(qh, kh, vt, lam_vecs, onorm_w, *, n_batch, seq, ctx_len, tq=512, tk=256, group=256,
               unroll=16):
    dva = DIFF_DV + ONES_ROWS
    n_keys = ctx_len + seq
    n_blk = n_keys // tk
    assert ctx_len % tk == 0 and seq % tq == 0 and unroll % 2 == 0 and (n_blk - 1) % unroll == 0
    nq = seq // tq
    return pl.pallas_call(
        functools.partial(_diff_attn_kernel, tq=tq, tk=tk, n_blk=n_blk, group=group, unroll=unroll),
        grid=(n_batch, DIFF_HEADS, nq),
        in_specs=[pl.BlockSpec((tq, LANES), lambda b, h, i: (b * nq + i, h)),
                  pl.BlockSpec((n_keys, LANES), lambda b, h, i: (b, h)),
                  pl.BlockSpec((n_blk, None, dva, tk), lambda b, h, i: (b, h, 0, 0)),
                  pl.BlockSpec((4, DIFF_DQK), lambda b, h, i: (0, 0)),
                  pl.BlockSpec((DIFF_DV, 1), lambda b, h, i: (0, 0))],
        out_specs=pl.BlockSpec((tq, DIFF_DV), lambda b, h, i: (b * nq + i, h)),
        out_shape=jax.ShapeDtypeStruct((n_batch * seq, DIFF_V), BF16),
        scratch_shapes=[pltpu.VMEM((2 * tq, LANES), BF16),
                        pltpu.VMEM((2, tk, 2 * tq), BF16),
                        pltpu.VMEM((2, tk, 2 * tq), BF16),
                        pltpu.VMEM((dva, 2 * tq), F32),
                        pltpu.VMEM((1, 2 * tq), F32),
                        pltpu.VMEM((1, 2 * tq), F32),
                        pltpu.VMEM((2, 1, 2 * tq), F32)],
        compiler_params=_params(3),
        name="diff_attn",
    )(qh, kh, vt, lam_vecs, onorm_w.reshape(DIFF_DV, 1))


def _conv_gate_kernel(ug_ref, uv_ref, pg_ref, pv_ref, ng_ref, nv_ref, wg_ref, wv_ref, bg_ref, bv_ref,
                      o_ref, *, tm, seq, halo):
    t0 = (pl.program_id(0) * tm) % seq
    first = t0 == 0
    last = t0 + tm == seq
    rid = lax.broadcasted_iota(jnp.int32, ug_ref.shape, 0)

    def conv(u_ref, p_ref, n_ref, w_ref, b_ref):
        u = u_ref[...].astype(F32)
        prow = jnp.where(first, 0.0, p_ref[halo - 1:halo, :].astype(F32))
        nrow = jnp.where(last, 0.0, n_ref[0:1, :].astype(F32))
        up = jnp.where(rid == 0, prow, pltpu.roll(u, 1, axis=0))
        un = jnp.where(rid == tm - 1, nrow, pltpu.roll(u, tm - 1, axis=0))
        w = w_ref[...]
        return up * w[0:1] + u * w[1:2] + un * w[2:3] + b_ref[...]

    cg = conv(ug_ref, pg_ref, ng_ref, wg_ref, bg_ref)
    cv = conv(uv_ref, pv_ref, nv_ref, wv_ref, bv_ref)
    o_ref[...] = (cg * _sigmoid(cg) * cv).astype(o_ref.dtype)


def _conv_gate(u, conv_w, conv_b, *, seq, tm=512, tn=512, halo=16):
    rows, n2 = u.shape
    dff = n2 // 2
    nj = dff // tn
    hb = tm // halo
    nhalo = rows // halo
    main = lambda off: pl.BlockSpec((tm, tn), lambda i, j: (i, off + j))
    prev = lambda off: pl.BlockSpec((halo, tn), lambda i, j: (jnp.maximum(i * hb - 1, 0), off + j))
    nxt = lambda off: pl.BlockSpec((halo, tn), lambda i, j: (jnp.minimum((i + 1) * hb, nhalo - 1), off + j))
    wsp = lambda off: pl.BlockSpec((CONV_W, tn), lambda i, j: (0, off + j))
    bsp = lambda off: pl.BlockSpec((1, tn), lambda i, j: (0, off + j))
    cb = conv_b.reshape(1, n2)
    return pl.pallas_call(
        functools.partial(_conv_gate_kernel, tm=tm, seq=seq, halo=halo),
        grid=(rows // tm, nj),
        in_specs=[main(0), main(nj), prev(0), prev(nj), nxt(0), nxt(nj), wsp(0), wsp(nj), bsp(0), bsp(nj)],
        out_specs=pl.BlockSpec((tm, tn), lambda i, j: (i, j)),
        out_shape=jax.ShapeDtypeStruct((rows, dff), BF16),
        compiler_params=_params(2),
        name="conv_gate",
    )(u, u, u, u, u, u, conv_w, conv_w, cb, cb)


def kernel(x, c, ctx, c_ctx, w_ada, b_ada, norm1_w, w_in, w_a_up_f, b_a_f, w_a_up_b, b_a_b, gla_onorm_w, diff_qnorm_w, diff_knorm_w, lambda_q1, lambda_k1, lambda_q2, lambda_k2, diff_onorm_w, w_proj_gla, w_proj_diff, w_gate, b_gate, w_out, norm2_w, w_up, conv_w, conv_b, w_down):
    n_batch, seq, d = x.shape
    ctx_len = ctx.shape[1]
    assert w_ada.shape[0] == 1, "single-layer block"
    rows = n_batch * seq
    crow = n_batch * ctx_len

    cs = jnp.concatenate([c, c_ctx[None, :], jnp.zeros((8 - n_batch - 1, d), F32)], axis=0)
    mod = _ada(cs, w_ada[0], b_ada[0])
    mod3 = mod.reshape(8, 1, 6 * d)

    o_r = 2 * GLA_QK + 2 * GLA_V
    o_d = o_r + 2 * GLA_RANK
    o_v = o_d + 2 * DIFF_QK
    w_i = w_in[0]
    w_main = jnp.concatenate([w_i[:, :o_r], w_i[:, o_d:o_v]], axis=1).astype(BF16)
    w_dv_t = w_i[:, o_v:].T.astype(BF16)
    w_lr = jnp.pad(w_i[:, o_r:o_d], ((0, 0), (0, LANES - 2 * GLA_RANK))).astype(BF16)
    w_upcat = jnp.zeros((LANES, 2 * GLA_QK), F32)
    w_upcat = w_upcat.at[:GLA_RANK, :GLA_QK].set(w_a_up_f[0])
    w_upcat = w_upcat.at[GLA_RANK:2 * GLA_RANK, GLA_QK:].set(w_a_up_b[0]).astype(BF16)
    b_upcat = jnp.concatenate([b_a_f[0], b_a_b[0]]).reshape(1, 2 * GLA_QK)

    x2 = x.reshape(rows, d)
    h_all = _norm1(x2, ctx.reshape(crow, d), norm1_w[0], mod3, seq=seq, n_batch=n_batch)
    all_rows = rows + crow
    qkv = _mm(h_all, w_main, rows=all_rows, tm=all_rows // 8, tn=1024, out_dtype=BF16, name="in_proj")
    v_t = _mm_vt(w_dv_t, h_all, n_batch=n_batch, seq=seq, tn=256)
    bcum = _decay(h_all, w_lr, w_upcat, b_upcat)

    s_zero = jnp.zeros((n_batch, GLA_HEADS // 2, 2 * GLA_DV, LANES), F32)
    seg_c = dict(row0=rows, seg_len=ctx_len, n_batch=n_batch, tb=ctx_len)
    seg_l = dict(row0=0, seg_len=seq, n_batch=n_batch, tb=1024)
    (s_cf,) = _gla(qkv, bcum, s_zero, reverse=False, write_o=False, **seg_c)
    (s_cb,) = _gla(qkv, bcum, s_zero, reverse=True, write_o=False, **seg_c)
    o_lf, _ = _gla(qkv, bcum, s_cf, reverse=False, write_o=True, o_rows=rows, **seg_l)
    y_a, _ = _gla(qkv, bcum, s_cb, reverse=True, write_o=True, o_rows=rows,
                  final=(o_lf, gla_onorm_w[0]), **seg_l)

    tk = 256
    tables = _rope_tables(seq, tk)
    q_col0 = o_r
    k_col0 = o_r + DIFF_QK
    tq_prep = 512
    qh = _qk_prep(qkv, diff_qnorm_w[0], tables, col0=q_col0, rows=rows, tm=tq_prep,
                  scale=(DIFF_DQK ** -0.5) * LOG2E, out_rows=rows,
                  table_map=lambda i: i % (seq // tq_prep), out_map=lambda i: i)
    lat_blk = seq // tk
    kh = _qk_prep(qkv, diff_knorm_w[0], tables, col0=k_col0, rows=all_rows, tm=tk, scale=1.0,
                  out_rows=all_rows,
                  table_map=lambda i: jnp.where(i < n_batch * lat_blk, i % lat_blk, lat_blk),
                  out_map=functools.partial(_kv_block_order, n_batch=n_batch, seq=seq, tn=tk))
    lam_vecs = jnp.stack([lambda_q1[0], lambda_k1[0], lambda_q2[0], lambda_k2[0]], axis=0)
    y_b = _diff_attn(qh, kh, v_t, lam_vecs, diff_onorm_w[0], n_batch=n_batch, seq=seq, ctx_len=ctx_len,
                     tk=tk)

    g = _mm_gate(h_all, w_gate[0].astype(BF16), b_gate[0], rows=rows)
    merged = _merge(y_a, y_b, w_proj_gla[0].astype(BF16), w_proj_diff[0].astype(BF16), g)
    x1 = _resid_mm(merged, w_out[0].astype(BF16), x2, mod3, gate_chunk=2, seq=seq,
                   tm=1024, tn=1024, tk=d, name="out_proj")

    h2 = _norm2(x1, norm2_w[0], mod3, seq=seq)
    u = _mm(h2, w_up[0].astype(BF16), rows=rows, tm=1024, tn=1024, out_dtype=BF16, name="ffn_up")
    act = _conv_gate(u, conv_w[0], conv_b[0], seq=seq)
    dff = act.shape[1]
    out = _resid_mm(act, w_down[0].astype(BF16), x1, mod3, gate_chunk=5, seq=seq,
                    tm=512, tn=1024, tk=dff // 2, name="ffn_down")
    return out.reshape(n_batch, seq, d)
```

```python
import functools
import math

import jax
import jax.numpy as jnp
import numpy as np
from jax import lax
from jax.experimental import pallas as pl
from jax.experimental.pallas import tpu as pltpu

F32 = jnp.float32
BF16 = jnp.bfloat16

GRID_W = 64
GLA_HEADS = 8
GLA_DK = 64
GLA_DV = 128
GLA_RANK = 16
GLA_TAU = 16.0
DIFF_HEADS = 8
DIFF_DQK = 64
DIFF_DV = 128
CONV_W = 3
ROPE_THETA = 10000.0
EPS = 1e-6
LAM_INIT = 0.8 - 0.6 * math.exp(-0.3 * 0)

GLA_QK = GLA_HEADS * GLA_DK
GLA_V = GLA_HEADS * GLA_DV
DIFF_QK = DIFF_HEADS * 2 * DIFF_DQK
DIFF_V = DIFF_HEADS * DIFF_DV

LANES = 128
VMEM_LIMIT = 56 * 1024 * 1024

GLA_CHUNK = 64
GLA_REFS = 4
GLA_UNROLL = 8
LOG2E = 1.4426950408889634


def _params(n_axes):
    return pltpu.CompilerParams(dimension_semantics=("arbitrary",) * n_axes,
                                vmem_limit_bytes=VMEM_LIMIT)


def _sigmoid(x):
    return 1.0 / (1.0 + jnp.exp(-x))


def _split_bf16(x):
    hi = x.astype(BF16)
    lo = (x - hi.astype(F32)).astype(BF16)
    return hi, lo


def _ada_kernel(c_ref, w_ref, b_ref, o_ref):
    c = c_ref[...]
    s = (c * _sigmoid(c)).astype(BF16)
    o_ref[...] = jnp.dot(s, w_ref[...].astype(BF16), preferred_element_type=F32) + b_ref[...]


def _ada(cs, w, b):
    d, n = w.shape
    tn = 1024
    return pl.pallas_call(
        _ada_kernel,
        grid=(n // tn,),
        in_specs=[pl.BlockSpec((8, d), lambda j: (0, 0)),
                  pl.BlockSpec((d, tn), lambda j: (0, j)),
                  pl.BlockSpec((1, tn), lambda j: (0, j))],
        out_specs=pl.BlockSpec((8, tn), lambda j: (0, j)),
        out_shape=jax.ShapeDtypeStruct((8, n), F32),
        compiler_params=_params(1),
        name="ada",
    )(cs, w, b.reshape(1, n))


def _norm_mod(x, nw, sh, sc):
    ms = jnp.mean(x * x, axis=-1, keepdims=True)
    y = x * lax.rsqrt(ms + EPS) * nw
    return (y * (1.0 + sc) + sh).astype(BF16)


def _norm1_kernel(x_ref, ctx_ref, nw_ref, sh_ref, sc_ref, o_ref, *, n_lat_tiles):
    i = pl.program_id(0)

    @pl.when(i < n_lat_tiles)
    def _():
        o_ref[...] = _norm_mod(x_ref[...], nw_ref[...], sh_ref[...], sc_ref[...])

    @pl.when(i >= n_lat_tiles)
    def _():
        o_ref[...] = _norm_mod(ctx_ref[...], nw_ref[...], sh_ref[...], sc_ref[...])


def _norm1(x2, ctx2, nw, mod3, *, seq, n_batch, tm=512):
    rows, d = x2.shape
    crow = ctx2.shape[0]
    assert crow == tm and rows % tm == 0 and seq % tm == 0
    nl = rows // tm
    per_b = seq // tm

    def mod_row(i):
        return jnp.where(i < nl, i // per_b, n_batch)

    return pl.pallas_call(
        functools.partial(_norm1_kernel, n_lat_tiles=nl),
        grid=(nl + 1,),
        in_specs=[pl.BlockSpec((tm, d), lambda i: (jnp.minimum(i, nl - 1), 0)),
                  pl.BlockSpec((tm, d), lambda i: (0, 0)),
                  pl.BlockSpec((1, d), lambda i: (0, 0)),
                  pl.BlockSpec((None, 1, d), lambda i: (mod_row(i), 0, 0)),
                  pl.BlockSpec((None, 1, d), lambda i: (mod_row(i), 0, 1))],
        out_specs=pl.BlockSpec((tm, d), lambda i: (i, 0)),
        out_shape=jax.ShapeDtypeStruct((rows + crow, d), BF16),
        compiler_params=_params(1),
        name="norm1",
    )(x2, ctx2, nw.reshape(1, d), mod3, mod3)


def _norm2_kernel(x_ref, nw_ref, sh_ref, sc_ref, o_ref):
    o_ref[...] = _norm_mod(x_ref[...], nw_ref[...], sh_ref[...], sc_ref[...])


def _norm2(x2, nw, mod3, *, seq, tm=512):
    rows, d = x2.shape
    per_b = seq // tm
    return pl.pallas_call(
        _norm2_kernel,
        grid=(rows // tm,),
        in_specs=[pl.BlockSpec((tm, d), lambda i: (i, 0)),
                  pl.BlockSpec((1, d), lambda i: (0, 0)),
                  pl.BlockSpec((None, 1, d), lambda i: (i // per_b, 0, 3)),
                  pl.BlockSpec((None, 1, d), lambda i: (i // per_b, 0, 4))],
        out_specs=pl.BlockSpec((tm, d), lambda i: (i, 0)),
        out_shape=jax.ShapeDtypeStruct((rows, d), BF16),
        compiler_params=_params(1),
        name="norm2",
    )(x2, nw.reshape(1, d), mod3, mod3)


def _resident_bf16(w_ref, wb_ref):
    @pl.when(pl.program_id(1) == 0)
    def _():
        wb_ref[...] = w_ref[...].astype(BF16)
    return wb_ref[...]


def _mm_kernel(x_ref, w_ref, *rest, gate):
    if gate:
        b_ref, o_ref, *scratch = rest
    else:
        o_ref, *scratch = rest
    w = _resident_bf16(w_ref, scratch[0]) if scratch else w_ref[...]
    z = jnp.dot(x_ref[...], w, preferred_element_type=F32)
    if gate:
        z = _sigmoid(z + b_ref[...])
    o_ref[...] = z.astype(o_ref.dtype)


def _mm(x, w, *, rows, tm, tn, n_cols, name, bias=None):
    k = w.shape[0]
    gate = bias is not None
    in_specs = [pl.BlockSpec((tm, k), lambda j, i: (i, 0)),
                pl.BlockSpec((k, tn), lambda j, i: (0, j))]
    args = [x, w]
    if gate:
        in_specs.append(pl.BlockSpec((1, tn), lambda j, i: (0, j)))
        args.append(bias.reshape(1, -1))
    return pl.pallas_call(
        functools.partial(_mm_kernel, gate=gate),
        grid=(n_cols // tn, rows // tm),
        in_specs=in_specs,
        out_specs=pl.BlockSpec((tm, tn), lambda j, i: (i, j)),
        out_shape=jax.ShapeDtypeStruct((rows, n_cols), BF16),
        scratch_shapes=[pltpu.VMEM((k, tn), BF16)] if w.dtype != BF16 else [],
        compiler_params=_params(2),
        name=name,
    )(*args)


ONES_ROWS = 16


def _vt_kernel(w_ref, x_ref, o_ref):
    vt = lax.dot_general(w_ref[...], x_ref[...], (((1,), (1,)), ((), ())),
                         preferred_element_type=F32).astype(o_ref.dtype)
    n_heads, _, tn = o_ref.shape
    for h in range(n_heads):
        o_ref[h, :DIFF_DV, :] = vt[h * DIFF_DV:(h + 1) * DIFF_DV]
        o_ref[h, DIFF_DV:, :] = jnp.ones((ONES_ROWS, tn), o_ref.dtype)


def _kv_block_order(j, *, n_batch, seq, tn):
    lat = seq // tn
    n_lat = n_batch * lat
    return jnp.where(j < n_lat, (j // lat) * (lat + 1) + 1 + j % lat, (j - n_lat) * (lat + 1))


def _mm_vt(wt, x, *, n_batch, seq, tn=256):
    n_out, k = wt.shape
    rows = x.shape[0]
    order = functools.partial(_kv_block_order, n_batch=n_batch, seq=seq, tn=tn)
    blk = (DIFF_HEADS, DIFF_DV + ONES_ROWS, tn)
    return pl.pallas_call(
        _vt_kernel,
        grid=(rows // tn,),
        in_specs=[pl.BlockSpec((n_out, k), lambda j: (0, 0)),
                  pl.BlockSpec((tn, k), lambda j: (j, 0))],
        out_specs=pl.BlockSpec((None,) + blk, lambda j: (order(j), 0, 0, 0)),
        out_shape=jax.ShapeDtypeStruct((rows // tn,) + blk, BF16),
        compiler_params=_params(1),
        name="v_transposed",
    )(wt, x)


def _merge_kernel(ya_ref, yb_ref, wa_ref, wb_ref, ga_ref, gb_ref, o_ref, wab_ref, wbb_ref):
    pa = jnp.dot(ya_ref[...], _resident_bf16(wa_ref, wab_ref), preferred_element_type=F32)
    pb = jnp.dot(yb_ref[...], _resident_bf16(wb_ref, wbb_ref), preferred_element_type=F32)
    o_ref[...] = (ga_ref[...].astype(F32) * pa + gb_ref[...].astype(F32) * pb).astype(o_ref.dtype)


def _merge(ya, yb, wa, wb, g, *, tm=1024, tn=1024):
    rows, ka = ya.shape
    kb = yb.shape[1]
    n = wa.shape[1]
    nb = n // tn
    return pl.pallas_call(
        _merge_kernel,
        grid=(nb, rows // tm),
        in_specs=[pl.BlockSpec((tm, ka), lambda j, i: (i, 0)),
                  pl.BlockSpec((tm, kb), lambda j, i: (i, 0)),
                  pl.BlockSpec((ka, tn), lambda j, i: (0, j)),
                  pl.BlockSpec((kb, tn), lambda j, i: (0, j)),
                  pl.BlockSpec((tm, tn), lambda j, i: (i, j)),
                  pl.BlockSpec((tm, tn), lambda j, i: (i, nb + j))],
        out_specs=pl.BlockSpec((tm, tn), lambda j, i: (i, j)),
        out_shape=jax.ShapeDtypeStruct((rows, n), BF16),
        scratch_shapes=[pltpu.VMEM((ka, tn), BF16), pltpu.VMEM((kb, tn), BF16)],
        compiler_params=_params(2),
        name="merge",
    )(ya, yb, wa, wb, g, g)


def _resid_kernel(a_ref, w_ref, x_ref, g_ref, o_ref, wb_ref):
    part = jnp.dot(a_ref[...], _resident_bf16(w_ref, wb_ref), preferred_element_type=F32)
    o_ref[...] = x_ref[...] + g_ref[...] * part


def _resid_mm(a, w, xres, mod3, *, gate_chunk, seq, tm, tn, name):
    rows, k = a.shape
    n = w.shape[1]
    per_b = seq // tm
    cpc = n // tn
    return pl.pallas_call(
        _resid_kernel,
        grid=(n // tn, rows // tm),
        in_specs=[pl.BlockSpec((tm, k), lambda j, i: (i, 0)),
                  pl.BlockSpec((k, tn), lambda j, i: (0, j)),
                  pl.BlockSpec((tm, tn), lambda j, i: (i, j)),
                  pl.BlockSpec((None, 1, tn), lambda j, i: (i // per_b, 0, gate_chunk * cpc + j))],
        out_specs=pl.BlockSpec((tm, tn), lambda j, i: (i, j)),
        out_shape=jax.ShapeDtypeStruct((rows, n), F32),
        scratch_shapes=[pltpu.VMEM((k, tn), BF16)],
        compiler_params=_params(2),
        name=name,
    )(a, w, xres, mod3)


def _decay_kernel(h_ref, wa_ref, wup_ref, bup_ref, tri_f_ref, tri_b_ref, o_ref, *, half):
    ga = jnp.dot(h_ref[...], wa_ref[...], preferred_element_type=F32)
    z = jnp.dot(ga.astype(BF16), wup_ref[...], preferred_element_type=F32) + bup_ref[...]
    la = (jnp.minimum(z, 0.0) - jnp.log1p(jnp.exp(-jnp.abs(z)))) * (1.0 / GLA_TAU)
    hi, lo = _split_bf16(la)
    for tri_ref, sl in ((tri_f_ref, slice(0, half)), (tri_b_ref, slice(half, 2 * half))):
        tri = tri_ref[...]
        o_ref[:, sl] = (jnp.dot(tri, hi[:, sl], preferred_element_type=F32)
                        + jnp.dot(tri, lo[:, sl], preferred_element_type=F32))


def _decay(h, wa, wup, bup, *, tm=512):
    rows, k = h.shape
    n = wup.shape[1]
    r = np.arange(tm)
    same = (r[:, None] // GLA_CHUNK) == (r[None, :] // GLA_CHUNK)
    tri_f = jnp.asarray(same & (r[None, :] <= r[:, None]), BF16)
    tri_b = jnp.asarray(same & (r[None, :] >= r[:, None]), BF16)
    return pl.pallas_call(
        functools.partial(_decay_kernel, half=n // 2),
        grid=(rows // tm,),
        in_specs=[pl.BlockSpec((tm, k), lambda i: (i, 0)),
                  pl.BlockSpec((k, LANES), lambda i: (0, 0)),
                  pl.BlockSpec((LANES, n), lambda i: (0, 0)),
                  pl.BlockSpec((1, n), lambda i: (0, 0)),
                  pl.BlockSpec((tm, tm), lambda i: (0, 0)),
                  pl.BlockSpec((tm, tm), lambda i: (0, 0))],
        out_specs=pl.BlockSpec((tm, n), lambda i: (i, 0)),
        out_shape=jax.ShapeDtypeStruct((rows, n), F32),
        compiler_params=_params(1),
        name="decay",
    )(h, wa, wup, bup, tri_f, tri_b)


def _gla_operands(q, k, b, *, reverse):
    c = GLA_CHUNK
    bs = c // GLA_REFS
    b_last = b[0:1, :] if reverse else b[c - 1:c, :]

    q_inter = q * jnp.exp(b)
    k_state = (k * jnp.exp(b_last - b)).astype(BF16)

    def rows_padded(t, lo):
        parts = [jnp.zeros((n, LANES), F32) for n in (lo,) if n] + [t]
        parts += [jnp.zeros((n, LANES), F32) for n in (c - lo - t.shape[0],) if n]
        return parts[0] if len(parts) == 1 else jnp.concatenate(parts, axis=0)

    q_segs, k_segs = [], []
    for r in range(GLA_REFS):
        lo, hi = r * bs, (r + 1) * bs
        m_r = b[lo:lo + 1, :] if reverse else b[hi - 1:hi, :]
        k_segs.append(rows_padded(k[lo:hi] * jnp.exp(m_r - b[lo:hi]), lo))
        if reverse:
            q_segs.append(rows_padded(q[:hi] * jnp.exp(b[:hi] - m_r), 0))
        else:
            q_segs.append(rows_padded(q[lo:] * jnp.exp(b[lo:] - m_r), lo))

    def per_head(t):
        ln = lax.broadcasted_iota(jnp.int32, t.shape, 1)
        ma = (ln % LANES) < GLA_DK
        return jnp.concatenate([jnp.where(ma, t, 0.0), jnp.where(ma, 0.0, t)], axis=0).astype(BF16)

    q_stack = per_head(jnp.concatenate(q_segs, axis=1))
    k_stack = jnp.concatenate(k_segs, axis=1).astype(BF16)
    return q_stack, k_stack, per_head(q_inter), k_state, jnp.exp(b_last)


def _gla_group(chunks, state_t, *, reverse):
    c = GLA_CHUNK
    dn_nt = (((1,), (1,)), ((), ()))
    ops = [_gla_operands(q, k, b, reverse=reverse) for q, k, _, b in chunks]
    scores = [lax.dot_general(o[0], o[1], dn_nt, preferred_element_type=F32) for o in ops]
    updates = [jnp.dot(ch[2].T, o[3], preferred_element_type=F32) for ch, o in zip(chunks, ops)]
    row = lax.broadcasted_iota(jnp.int32, (2 * c, c), 0) % c
    col = lax.broadcasted_iota(jnp.int32, (2 * c, c), 1)
    keep = (col >= row) if reverse else (col <= row)
    intra = []
    for ch, s in zip(chunks, scores):
        p = jnp.where(keep, s, 0.0).astype(BF16)
        v = ch[2]
        intra.append((jnp.dot(p[:c], v[:, :GLA_DV], preferred_element_type=F32),
                      jnp.dot(p[c:], v[:, GLA_DV:], preferred_element_type=F32)))
    outs = []
    for o, (ia, ib), upd in zip(ops, intra, updates):
        st = state_t.astype(BF16)
        qi = o[2]
        o_a = ia + lax.dot_general(qi[:c], st[:GLA_DV], dn_nt, preferred_element_type=F32)
        o_b = ib + lax.dot_general(qi[c:], st[GLA_DV:], dn_nt, preferred_element_type=F32)
        outs.append(jnp.concatenate([o_a, o_b], axis=1))
        state_t = o[4] * state_t + upd
    return outs, state_t


def _gla_kernel(*refs, reverse, n_chunks, unroll, write_o, final):
    q_ref, k_ref, v_ref, b_ref, s0_ref = refs[:5]
    pos = 5
    if final:
        of_ref, r_ref, nw_ref = refs[pos:pos + 3]
        pos += 3
    if write_o:
        o_ref = refs[pos]
        pos += 1
    sout_ref, state_ref = refs[pos], refs[pos + 1]
    j = pl.program_id(2)
    c = GLA_CHUNK

    @pl.when(j == 0)
    def _():
        state_ref[...] = s0_ref[...]

    def body(trip, carry):
        starts, chunks = [], []
        for u in range(unroll):
            step = trip * unroll + u
            ci = (n_chunks - 1 - step) if reverse else step
            r0 = pl.multiple_of(ci * c, c)
            starts.append(r0)
            chunks.append((q_ref[pl.ds(r0, c), :].astype(F32) * (GLA_DK ** -0.5),
                           k_ref[pl.ds(r0, c), :].astype(F32),
                           v_ref[pl.ds(r0, c), :],
                           b_ref[pl.ds(r0, c), :]))
        outs, new_state = _gla_group(chunks, state_ref[...], reverse=reverse)
        state_ref[...] = new_state
        for r0, o in zip(starts, outs):
            emit(r0, o)
        return carry

    def emit(r0, o):
        if final:
            o = o + of_ref[pl.ds(r0, c), :]
            gate = r_ref[pl.ds(r0, c), :].astype(F32)
            gate = gate * _sigmoid(gate)
            nw = nw_ref[...]
            ys = []
            for h in range(2):
                oh = o[:, h * GLA_DV:(h + 1) * GLA_DV]
                ms = jnp.mean(oh * oh, axis=-1, keepdims=True)
                ys.append(oh * lax.rsqrt(ms + EPS) * nw)
            o_ref[pl.ds(r0, c), :] = (jnp.concatenate(ys, axis=1) * gate).astype(o_ref.dtype)
        elif write_o:
            o_ref[pl.ds(r0, c), :] = o

    lax.fori_loop(0, n_chunks // unroll, body, 0)

    @pl.when(j == pl.num_programs(2) - 1)
    def _():
        sout_ref[...] = state_ref[...]


def _gla(qkv, bcum, s0, *, row0, seg_len, n_batch, tb, reverse, write_o, o_rows=0, final=None):
    n_pairs = GLA_HEADS // 2
    nblk = seg_len // tb
    assert row0 % tb == 0 and seg_len % tb == 0 and tb % GLA_CHUNK == 0
    qw, vw = LANES, 2 * GLA_DV
    k_col0 = GLA_QK // qw
    v_col0 = (2 * GLA_QK) // vw
    r_col0 = (2 * GLA_QK + GLA_V) // vw
    b_col0 = (GLA_QK // qw) if reverse else 0

    def blk(b, j):
        jj = (nblk - 1 - j) if reverse else j
        return row0 // tb + b * nblk + jj

    def oblk(b, j):
        jj = (nblk - 1 - j) if reverse else j
        return b * nblk + jj

    in_specs = [pl.BlockSpec((tb, qw), lambda b, p, j: (blk(b, j), p)),
                pl.BlockSpec((tb, qw), lambda b, p, j: (blk(b, j), k_col0 + p)),
                pl.BlockSpec((tb, vw), lambda b, p, j: (blk(b, j), v_col0 + p)),
                pl.BlockSpec((tb, qw), lambda b, p, j: (blk(b, j), b_col0 + p)),
                pl.BlockSpec((None, None, vw, qw), lambda b, p, j: (b, p, 0, 0))]
    args = [qkv, qkv, qkv, bcum, s0]
    if final is not None:
        o_fwd, onw = final
        in_specs += [pl.BlockSpec((tb, vw), lambda b, p, j: (oblk(b, j), p)),
                     pl.BlockSpec((tb, vw), lambda b, p, j: (blk(b, j), r_col0 + p)),
                     pl.BlockSpec((1, GLA_DV), lambda b, p, j: (0, 0))]
        args += [o_fwd, qkv, onw.reshape(1, GLA_DV)]
    out_specs, out_shape = [], []
    if write_o:
        out_specs.append(pl.BlockSpec((tb, vw), lambda b, p, j: (oblk(b, j), p)))
        out_shape.append(jax.ShapeDtypeStruct((o_rows, GLA_V), BF16 if final is not None else F32))
    out_specs.append(pl.BlockSpec((None, None, vw, qw), lambda b, p, j: (b, p, 0, 0)))
    out_shape.append(jax.ShapeDtypeStruct(s0.shape, F32))
    outs = pl.pallas_call(
        functools.partial(_gla_kernel, reverse=reverse, n_chunks=tb // GLA_CHUNK,
                          unroll=min(GLA_UNROLL, tb // GLA_CHUNK), write_o=write_o, final=final is not None),
        grid=(n_batch, n_pairs, nblk),
        in_specs=in_specs,
        out_specs=out_specs,
        out_shape=out_shape,
        scratch_shapes=[pltpu.VMEM((vw, qw), F32)],
        compiler_params=_params(3),
        name="gla_" + ("bwd" if reverse else "fwd") + ("_o" if write_o else "_state"),
    )(*args)
    return outs


def _qk_prep_kernel(x_ref, w_ref, g_ref, cos_ref, sa_ref, sb_ref, o_ref, *, scale, n_heads):
    g = g_ref[...]
    w = w_ref[...]
    cos, sa, sb = cos_ref[...], sa_ref[...], sb_ref[...]
    for h in range(n_heads):
        sl = slice(h * LANES, (h + 1) * LANES)
        x = x_ref[:, sl].astype(F32)
        hi, lo = _split_bf16(x * x)
        ms = (jnp.dot(hi, g, preferred_element_type=F32) + jnp.dot(lo, g, preferred_element_type=F32))
        y = x * lax.rsqrt(ms + EPS) * w
        half = DIFF_DQK // 4
        y = y * cos + pltpu.roll(y, LANES - half, axis=1) * sa + pltpu.roll(y, half, axis=1) * sb
        o_ref[:, sl] = (y * scale).astype(o_ref.dtype)


def _qk_prep(qkv, w, tables, *, col0, rows, tm, scale, table_map, out_map, out_rows):
    n = DIFF_QK
    cos_t, sa_t, sb_t = tables
    lane = np.arange(LANES)
    gmat = jnp.asarray((lane[:, None] // DIFF_DQK == lane[None, :] // DIFF_DQK) / DIFF_DQK, BF16)
    w2 = jnp.tile(w.reshape(1, DIFF_DQK), (1, LANES // DIFF_DQK))
    tspec = pl.BlockSpec((tm, LANES), lambda i: (table_map(i), 0))
    return pl.pallas_call(
        functools.partial(_qk_prep_kernel, scale=scale, n_heads=DIFF_HEADS),
        grid=(rows // tm,),
        in_specs=[pl.BlockSpec((tm, n), lambda i: (i, col0 // n)),
                  pl.BlockSpec((1, LANES), lambda i: (0, 0)),
                  pl.BlockSpec((LANES, LANES), lambda i: (0, 0)),
                  tspec, tspec, tspec],
        out_specs=pl.BlockSpec((tm, n), lambda i: (out_map(i), 0)),
        out_shape=jax.ShapeDtypeStruct((out_rows, n), BF16),
        compiler_params=_params(1),
        name="qk_prep",
    )(qkv, w2, gmat, cos_t, sa_t, sb_t)


def _rope_tables(seq, ctx_rows):
    n_freq = DIFF_DQK // 4
    t = jnp.arange(seq)
    inv_freq = ROPE_THETA ** (-jnp.arange(n_freq, dtype=F32) / n_freq)
    ang_row = (t // GRID_W).astype(F32)[:, None] * inv_freq
    ang_col = (t % GRID_W).astype(F32)[:, None] * inv_freq
    lane = np.arange(LANES)
    use_col = (lane % DIFF_DQK) >= DIFF_DQK // 2
    first_half = (lane % (2 * n_freq)) < n_freq
    ang = jnp.where(use_col[None, :], ang_col[:, lane % n_freq], ang_row[:, lane % n_freq])
    cos, sin = jnp.cos(ang), jnp.sin(ang)
    sa = jnp.where(first_half[None, :], -sin, 0.0)
    sb = jnp.where(first_half[None, :], 0.0, sin)
    pad = lambda a, v: jnp.concatenate([a, jnp.full((ctx_rows, LANES), v, F32)], axis=0)
    return pad(cos, 1.0), pad(sa, 0.0), pad(sb, 0.0)


def _diff_attn_kernel(q_ref, k_ref, v_ref, lam_ref, w_ref, o_ref,
                      qs_ref, s_ref, p_ref, acc_ref, m_ref, alpha_ref, bmax_ref, *, tq, tk, n_blk, group,
                      unroll):
    q = q_ref[...]
    lane = lax.broadcasted_iota(jnp.int32, q.shape, 1)
    zero = jnp.zeros_like(q)
    first = lane < DIFF_DQK
    qs_ref[...] = jnp.concatenate([jnp.where(first, q, zero), jnp.where(first, zero, q)], axis=0)
    width = 2 * tq

    def scores(i, slot):
        r = pl.multiple_of(i * tk, tk)
        kblk = k_ref[pl.ds(r, tk), :]
        for g0 in range(0, width, group):
            cols = slice(g0, g0 + group)
            s = lax.dot_general(kblk, qs_ref[cols, :], (((1,), (1,)), ((), ())),
                                preferred_element_type=F32).astype(BF16)
            s_ref[slot, :, cols] = s
            bmax_ref[slot, :, cols] = jnp.max(s, axis=0, keepdims=True).astype(F32)

    def softmax(slot):
        for g0 in range(0, width, group):
            cols = slice(g0, g0 + group)
            m_old = m_ref[:, cols]
            m_new = jnp.maximum(m_old, bmax_ref[slot, :, cols])
            m_ref[:, cols] = m_new
            alpha_ref[:, cols] = jnp.exp2(m_old - m_new)
            p_ref[slot, :, cols] = jnp.exp2(s_ref[slot, :, cols] - m_new.astype(BF16))

    def values(i, slot):
        acc_ref[...] = (alpha_ref[...] * acc_ref[...]
                        + jnp.dot(v_ref[i], p_ref[slot], preferred_element_type=F32))

    acc_ref[...] = jnp.zeros_like(acc_ref)
    m_ref[...] = jnp.full(m_ref.shape, -1e30, F32)
    scores(0, 0)
    scores(1, 1)
    softmax(0)

    def body(j, carry):
        for u in range(unroll):
            blk = 1 + unroll * j + u
            slot = (1 + u) % 2
            scores(jnp.minimum(blk + 1, n_blk - 1), 1 - slot)
            values(blk - 1, 1 - slot)
            softmax(slot)
        return carry

    lax.fori_loop(0, (n_blk - 1) // unroll, body, 0)
    values(n_blk - 1, 0)

    lam_v = lam_ref[...]
    lam = (jnp.exp(jnp.sum(lam_v[0:1] * lam_v[1:2], keepdims=True))
           - jnp.exp(jnp.sum(lam_v[2:3] * lam_v[3:4], keepdims=True)) + LAM_INIT)
    acc = acc_ref[:DIFF_DV, :]
    l = acc_ref[DIFF_DV:DIFF_DV + 1, :]
    o_t = acc[:, :tq] / l[:, :tq] - lam * (acc[:, tq:] / l[:, tq:])
    ms = jnp.mean(o_t * o_t, axis=0, keepdims=True)
    y = o_t * lax.rsqrt(ms + EPS) * w_ref[...] * (1.0 - LAM_INIT)
    o_ref[...] = y.T.astype(o_ref.dtype)


def _diff_attn(qh, kh, vt, lam_vecs, onorm_w, *, n_batch, seq, ctx_len, tq=512, tk=256, group=256,
               unroll=16):
    dva = DIFF_DV + ONES_ROWS
    n_keys = ctx_len + seq
    n_blk = n_keys // tk
    assert ctx_len % tk == 0 and seq % tq == 0 and unroll % 2 == 0 and (n_blk - 1) % unroll == 0
    nq = seq // tq
    return pl.pallas_call(
        functools.partial(_diff_attn_kernel, tq=tq, tk=tk, n_blk=n_blk, group=group, unroll=unroll),
        grid=(n_batch, DIFF_HEADS, nq),
        in_specs=[pl.BlockSpec((tq, LANES), lambda b, h, i: (b * nq + i, h)),
                  pl.BlockSpec((n_keys, LANES), lambda b, h, i: (b, h)),
                  pl.BlockSpec((n_blk, None, dva, tk), lambda b, h, i: (b, h, 0, 0)),
                  pl.BlockSpec((4, DIFF_DQK), lambda b, h, i: (0, 0)),
                  pl.BlockSpec((DIFF_DV, 1), lambda b, h, i: (0, 0))],
        out_specs=pl.BlockSpec((tq, DIFF_DV), lambda b, h, i: (b * nq + i, h)),
        out_shape=jax.ShapeDtypeStruct((n_batch * seq, DIFF_V), BF16),
        scratch_shapes=[pltpu.VMEM((2 * tq, LANES), BF16),
                        pltpu.VMEM((2, tk, 2 * tq), BF16),
                        pltpu.VMEM((2, tk, 2 * tq), BF16),
                        pltpu.VMEM((dva, 2 * tq), F32),
                        pltpu.VMEM((1, 2 * tq), F32),
                        pltpu.VMEM((1, 2 * tq), F32),
                        pltpu.VMEM((2, 1, 2 * tq), F32)],
        compiler_params=_params(3),
        name="diff_attn",
    )(qh, kh, vt, lam_vecs, onorm_w.reshape(DIFF_DV, 1))


HALO = 8


def _ffn_up_kernel(x_ref, hb_ref, wg_ref, wv_ref, cwg_ref, cwv_ref, cbg_ref, cbv_ref, o_ref,
                   wgb_ref, wvb_ref, ub_ref, ug_ref, uv_ref, *, tm, seq, sub):
    i = pl.program_id(1)
    tn = o_ref.shape[1]

    @pl.when(i == 0)
    def _():
        wgb_ref[...] = wg_ref[...].astype(BF16)
        wvb_ref[...] = wv_ref[...].astype(BF16)
        hb = hb_ref[...]
        ub_ref[:, :tn] = jnp.dot(hb, wgb_ref[...], preferred_element_type=F32)
        ub_ref[:, tn:] = jnp.dot(hb, wvb_ref[...], preferred_element_type=F32)

    t0 = (i * tm) % seq
    first = t0 == 0
    last = t0 + tm == seq
    bnd = ub_ref[pl.ds(pl.multiple_of(i * HALO, HALO), HALO), :]
    streams = ((wgb_ref, ug_ref, cwg_ref, cbg_ref, 0), (wvb_ref, uv_ref, cwv_ref, cbv_ref, tn))
    for _, u_ref, _, _, c0 in streams:
        u_ref[HALO - 1:HALO, :] = jnp.where(first, 0.0, bnd[0:1, c0:c0 + tn])
        u_ref[HALO + tm:HALO + tm + 1, :] = jnp.where(last, 0.0, bnd[1:2, c0:c0 + tn])

    def project(r):
        for wb_ref, u_ref, _, _, _ in streams:
            u_ref[HALO + r * sub:HALO + (r + 1) * sub, :] = jnp.dot(
                x_ref[r * sub:(r + 1) * sub, :], wb_ref[...], preferred_element_type=F32)

    def activate(r):
        lo = HALO + r * sub
        conv = []
        for _, u_ref, cw_ref, cb_ref, _ in streams:
            cw = cw_ref[...]
            conv.append(u_ref[lo - 1:lo - 1 + sub, :] * cw[0:1] + u_ref[lo:lo + sub, :] * cw[1:2]
                        + u_ref[lo + 1:lo + 1 + sub, :] * cw[2:3] + cb_ref[...])
        cg, cv = conv
        o_ref[r * sub:(r + 1) * sub, :] = (cg * _sigmoid(cg) * cv).astype(o_ref.dtype)

    n_sub = tm // sub
    project(0)
    for r in range(1, n_sub):
        project(r)
        activate(r - 1)
    activate(n_sub - 1)


def _ffn_up(h2, w_up, conv_w, conv_b, *, seq, tm=1024, tn=512, sub=512):
    rows, k = h2.shape
    dff = w_up.shape[1] // 2
    nj = dff // tn
    nt = rows // tm
    tiles = h2.reshape(nt, tm, k)
    zero = jnp.zeros((1, k), h2.dtype)
    before = jnp.concatenate([zero, tiles[:-1, tm - 1]], axis=0)
    after = jnp.concatenate([tiles[1:, 0], zero], axis=0)
    hb = jnp.concatenate([before[:, None], after[:, None], jnp.zeros((nt, HALO - 2, k), h2.dtype)],
                         axis=1).reshape(nt * HALO, k)
    wsp = lambda off: pl.BlockSpec((k, tn), lambda j, i: (0, off + j))
    csp = lambda off: pl.BlockSpec((CONV_W, tn), lambda j, i: (0, off + j))
    bsp = lambda off: pl.BlockSpec((1, tn), lambda j, i: (0, off + j))
    cb = conv_b.reshape(1, 2 * dff)
    return pl.pallas_call(
        functools.partial(_ffn_up_kernel, tm=tm, seq=seq, sub=sub),
        grid=(nj, nt),
        in_specs=[pl.BlockSpec((tm, k), lambda j, i: (i, 0)),
                  pl.BlockSpec((nt * HALO, k), lambda j, i: (0, 0)),
                  wsp(0), wsp(nj), csp(0), csp(nj), bsp(0), bsp(nj)],
        out_specs=pl.BlockSpec((tm, tn), lambda j, i: (i, j)),
        out_shape=jax.ShapeDtypeStruct((rows, dff), BF16),
        scratch_shapes=[pltpu.VMEM((k, tn), BF16), pltpu.VMEM((k, tn), BF16),
                        pltpu.VMEM((nt * HALO, 2 * tn), F32),
                        pltpu.VMEM((tm + 2 * HALO, tn), F32), pltpu.VMEM((tm + 2 * HALO, tn), F32)],
        compiler_params=_params(2),
        name="ffn_up",
    )(h2, hb, w_up, w_up, conv_w, conv_w, cb, cb)


def kernel(x, c, ctx, c_ctx, w_ada, b_ada, norm1_w, w_in, w_a_up_f, b_a_f, w_a_up_b, b_a_b, gla_onorm_w, diff_qnorm_w, diff_knorm_w, lambda_q1, lambda_k1, lambda_q2, lambda_k2, diff_onorm_w, w_proj_gla, w_proj_diff, w_gate, b_gate, w_out, norm2_w, w_up, conv_w, conv_b, w_down):
    n_batch, seq, d = x.shape
    ctx_len = ctx.shape[1]
    assert w_ada.shape[0] == 1, "single-layer block"
    rows = n_batch * seq
    crow = n_batch * ctx_len

    cs = jnp.concatenate([c, c_ctx[None, :], jnp.zeros((8 - n_batch - 1, d), F32)], axis=0)
    mod = _ada(cs, w_ada[0], b_ada[0])
    mod3 = mod.reshape(8, 1, 6 * d)

    o_r = 2 * GLA_QK + 2 * GLA_V
    o_d = o_r + 2 * GLA_RANK
    o_v = o_d + 2 * DIFF_QK
    w_i = w_in[0]
    w_dqk = w_i[:, o_d:o_v].astype(BF16)
    w_dv_t = w_i[:, o_v:].T.astype(BF16)
    w_lr = jnp.pad(w_i[:, o_r:o_d], ((0, 0), (0, LANES - 2 * GLA_RANK))).astype(BF16)
    w_upcat = jnp.zeros((LANES, 2 * GLA_QK), F32)
    w_upcat = w_upcat.at[:GLA_RANK, :GLA_QK].set(w_a_up_f[0])
    w_upcat = w_upcat.at[GLA_RANK:2 * GLA_RANK, GLA_QK:].set(w_a_up_b[0]).astype(BF16)
    b_upcat = jnp.concatenate([b_a_f[0], b_a_b[0]]).reshape(1, 2 * GLA_QK)

    x2 = x.reshape(rows, d)
    h_all = _norm1(x2, ctx.reshape(crow, d), norm1_w[0], mod3, seq=seq, n_batch=n_batch)
    all_rows = rows + crow
    qkv = _mm(h_all, w_i, rows=all_rows, tm=all_rows // 8, tn=1024, n_cols=o_r, name="in_proj_gla")
    dqk = _mm(h_all, w_dqk, rows=all_rows, tm=all_rows // 8, tn=1024, n_cols=2 * DIFF_QK,
              name="in_proj_diff")
    v_t = _mm_vt(w_dv_t, h_all, n_batch=n_batch, seq=seq, tn=256)
    bcum = _decay(h_all, w_lr, w_upcat, b_upcat)

    s_zero = jnp.zeros((n_batch, GLA_HEADS // 2, 2 * GLA_DV, LANES), F32)
    seg_c = dict(row0=rows, seg_len=ctx_len, n_batch=n_batch, tb=ctx_len)
    seg_l = dict(row0=0, seg_len=seq, n_batch=n_batch, tb=1024)
    (s_cf,) = _gla(qkv, bcum, s_zero, reverse=False, write_o=False, **seg_c)
    (s_cb,) = _gla(qkv, bcum, s_zero, reverse=True, write_o=False, **seg_c)
    o_lf, _ = _gla(qkv, bcum, s_cf, reverse=False, write_o=True, o_rows=rows, **seg_l)
    y_a, _ = _gla(qkv, bcum, s_cb, reverse=True, write_o=True, o_rows=rows,
                  final=(o_lf, gla_onorm_w[0]), **seg_l)

    tk = 256
    tables = _rope_tables(seq, tk)
    q_col0 = 0
    k_col0 = DIFF_QK
    tq_prep = 512
    qh = _qk_prep(dqk, diff_qnorm_w[0], tables, col0=q_col0, rows=rows, tm=tq_prep,
                  scale=(DIFF_DQK ** -0.5) * LOG2E, out_rows=rows,
                  table_map=lambda i: i % (seq // tq_prep), out_map=lambda i: i)
    lat_blk = seq // tk
    kh = _qk_prep(dqk, diff_knorm_w[0], tables, col0=k_col0, rows=all_rows, tm=tk, scale=1.0,
                  out_rows=all_rows,
                  table_map=lambda i: jnp.where(i < n_batch * lat_blk, i % lat_blk, lat_blk),
                  out_map=functools.partial(_kv_block_order, n_batch=n_batch, seq=seq, tn=tk))
    lam_vecs = jnp.stack([lambda_q1[0], lambda_k1[0], lambda_q2[0], lambda_k2[0]], axis=0)
    y_b = _diff_attn(qh, kh, v_t, lam_vecs, diff_onorm_w[0], n_batch=n_batch, seq=seq, ctx_len=ctx_len,
                     tk=tk)

    g = _mm(h_all, w_gate[0], rows=rows, tm=1024, tn=1024, n_cols=2 * d, name="gate", bias=b_gate[0])
    merged = _merge(y_a, y_b, w_proj_gla[0], w_proj_diff[0], g)
    x1 = _resid_mm(merged, w_out[0], x2, mod3, gate_chunk=2, seq=seq, tm=1024, tn=1024, name="out_proj")

    h2 = _norm2(x1, norm2_w[0], mod3, seq=seq)
    act = _ffn_up(h2, w_up[0], conv_w[0], conv_b[0], seq=seq)
    out = _resid_mm(act, w_down[0], x1, mod3, gate_chunk=5, seq=seq, tm=512, tn=512, name="ffn_down")
    return out.reshape(n_batch, seq, d)
```

```python
import functools
import math

import jax
import jax.numpy as jnp
import numpy as np
from jax import lax
from jax.experimental import pallas as pl
from jax.experimental.pallas import tpu as pltpu

F32 = jnp.float32
BF16 = jnp.bfloat16

GRID_W = 64
GLA_HEADS = 8
GLA_DK = 64
GLA_DV = 128
GLA_RANK = 16
GLA_TAU = 16.0
DIFF_HEADS = 8
DIFF_DQK = 64
DIFF_DV = 128
CONV_W = 3
ROPE_THETA = 10000.0
EPS = 1e-6
LAM_INIT = 0.8 - 0.6 * math.exp(-0.3 * 0)

GLA_QK = GLA_HEADS * GLA_DK
GLA_V = GLA_HEADS * GLA_DV
DIFF_QK = DIFF_HEADS * 2 * DIFF_DQK
DIFF_V = DIFF_HEADS * DIFF_DV

LANES = 128
V7X_VMEM_BYTES = 64 * 1024 * 1024
VMEM_LIMIT = V7X_VMEM_BYTES - 8 * 1024 * 1024

GLA_CHUNK = 64
GLA_REFS = 4
GLA_UNROLL = 16
LOG2E = 1.4426950408889634

TILES = dict(
    norm=512,
    dense=dict(tm=1024, tn=1024),
    in_proj=dict(tn=1024),
    ffn_up=dict(tm=1024, tn=512, sub=256),
    ffn_down=dict(tm=512, tn=512),
    gla_rows=1024,
    attention=dict(tq=512, tk=256, group=256),
)


def _params(n_axes):
    return pltpu.CompilerParams(dimension_semantics=("arbitrary",) * n_axes,
                                vmem_limit_bytes=VMEM_LIMIT)


def _sigmoid(x):
    return 1.0 / (1.0 + jnp.exp2(x * (-LOG2E)))


def _split_bf16(x):
    hi = x.astype(BF16)
    lo = (x - hi.astype(F32)).astype(BF16)
    return hi, lo


def _ada_kernel(c_ref, w_ref, b_ref, o_ref):
    c = c_ref[...]
    s = (c * _sigmoid(c)).astype(BF16)
    o_ref[...] = jnp.dot(s, w_ref[...].astype(BF16), preferred_element_type=F32) + b_ref[...]


def _ada(cs, w, b):
    d, n = w.shape
    tn = TILES["dense"]["tn"]
    return pl.pallas_call(
        _ada_kernel,
        grid=(n // tn,),
        in_specs=[pl.BlockSpec((8, d), lambda j: (0, 0)),
                  pl.BlockSpec((d, tn), lambda j: (0, j)),
                  pl.BlockSpec((1, tn), lambda j: (0, j))],
        out_specs=pl.BlockSpec((8, tn), lambda j: (0, j)),
        out_shape=jax.ShapeDtypeStruct((8, n), F32),
        compiler_params=_params(1),
        name="ada",
    )(cs, w, b.reshape(1, n))


NORM_ROWS = 16


def _norm_mod(x_ref, nw_ref, sh_ref, sc_ref, o_ref):
    gain = nw_ref[...] * (1.0 + sc_ref[...])
    shift = sh_ref[...]

    def body(r, carry):
        rows = pl.ds(pl.multiple_of(r * NORM_ROWS, NORM_ROWS), NORM_ROWS)
        x = x_ref[rows, :]
        ms = jnp.mean(x * x, axis=-1, keepdims=True)
        o_ref[rows, :] = (x * lax.rsqrt(ms + EPS) * gain + shift).astype(o_ref.dtype)
        return carry

    lax.fori_loop(0, x_ref.shape[0] // NORM_ROWS, body, 0, unroll=4)


def _norm1_kernel(x_ref, ctx_ref, nw_ref, sh_ref, sc_ref, o_ref, *, n_lat_tiles):
    i = pl.program_id(0)

    @pl.when(i < n_lat_tiles)
    def _():
        _norm_mod(x_ref, nw_ref, sh_ref, sc_ref, o_ref)

    @pl.when(i >= n_lat_tiles)
    def _():
        _norm_mod(ctx_ref, nw_ref, sh_ref, sc_ref, o_ref)


def _norm1(x2, ctx2, nw, mod3, *, seq, n_batch, tm):
    rows, d = x2.shape
    crow = ctx2.shape[0]
    assert crow == tm and rows % tm == 0 and seq % tm == 0
    nl = rows // tm
    per_b = seq // tm

    def mod_row(i):
        return jnp.where(i < nl, i // per_b, n_batch)

    return pl.pallas_call(
        functools.partial(_norm1_kernel, n_lat_tiles=nl),
        grid=(nl + 1,),
        in_specs=[pl.BlockSpec((tm, d), lambda i: (jnp.minimum(i, nl - 1), 0)),
                  pl.BlockSpec((tm, d), lambda i: (0, 0)),
                  pl.BlockSpec((1, d), lambda i: (0, 0)),
                  pl.BlockSpec((None, 1, d), lambda i: (mod_row(i), 0, 0)),
                  pl.BlockSpec((None, 1, d), lambda i: (mod_row(i), 0, 1))],
        out_specs=pl.BlockSpec((tm, d), lambda i: (i, 0)),
        out_shape=jax.ShapeDtypeStruct((rows + crow, d), BF16),
        compiler_params=_params(1),
        name="norm1",
    )(x2, ctx2, nw.reshape(1, d), mod3, mod3)


def _norm2_kernel(x_ref, nw_ref, sh_ref, sc_ref, o_ref):
    _norm_mod(x_ref, nw_ref, sh_ref, sc_ref, o_ref)


def _norm2(x2, nw, mod3, *, seq, tm):
    rows, d = x2.shape
    per_b = seq // tm
    return pl.pallas_call(
        _norm2_kernel,
        grid=(rows // tm,),
        in_specs=[pl.BlockSpec((tm, d), lambda i: (i, 0)),
                  pl.BlockSpec((1, d), lambda i: (0, 0)),
                  pl.BlockSpec((None, 1, d), lambda i: (i // per_b, 0, 3)),
                  pl.BlockSpec((None, 1, d), lambda i: (i // per_b, 0, 4))],
        out_specs=pl.BlockSpec((tm, d), lambda i: (i, 0)),
        out_shape=jax.ShapeDtypeStruct((rows, d), BF16),
        compiler_params=_params(1),
        name="norm2",
    )(x2, nw.reshape(1, d), mod3, mod3)


def _resident_bf16(w_ref, wb_ref):
    @pl.when(pl.program_id(1) == 0)
    def _():
        wb_ref[...] = w_ref[...].astype(BF16)
    return wb_ref[...]


def _mm_kernel(x_ref, w_ref, *rest, gate, shift):
    if shift:
        tail_ref, *rest = rest
    if gate:
        b_ref, o_ref, *scratch = rest
    else:
        o_ref, *scratch = rest
    if shift:
        @pl.when(pl.program_id(1) == 0)
        def _():
            wide = jnp.concatenate([w_ref[...], tail_ref[...]], axis=1)
            scratch[0][...] = wide[:, shift:shift + w_ref.shape[1]].astype(BF16)
        w = scratch[0][...]
    else:
        w = _resident_bf16(w_ref, scratch[0]) if scratch else w_ref[...]
    z = jnp.dot(x_ref[...], w, preferred_element_type=F32)
    if gate:
        z = _sigmoid(z + b_ref[...])
    o_ref[...] = z.astype(o_ref.dtype)


def _mm(x, w, *, rows, tm, tn, n_cols, name, bias=None, col0=0, shift=0):
    k = w.shape[0]
    gate = bias is not None
    assert col0 % tn == 0 and 0 <= shift < LANES
    in_specs = [pl.BlockSpec((tm, k), lambda j, i: (i, 0)),
                pl.BlockSpec((k, tn), lambda j, i: (0, col0 // tn + j))]
    args = [x, w]
    if shift:
        in_specs.append(pl.BlockSpec((k, LANES), lambda j, i: (0, (col0 + (j + 1) * tn) // LANES)))
        args.append(w)
    if gate:
        in_specs.append(pl.BlockSpec((1, tn), lambda j, i: (0, j)))
        args.append(bias.reshape(1, -1))
    return pl.pallas_call(
        functools.partial(_mm_kernel, gate=gate, shift=shift),
        grid=(n_cols // tn, rows // tm),
        in_specs=in_specs,
        out_specs=pl.BlockSpec((tm, tn), lambda j, i: (i, j)),
        out_shape=jax.ShapeDtypeStruct((rows, n_cols), BF16),
        scratch_shapes=[pltpu.VMEM((k, tn), BF16)] if (w.dtype != BF16 or shift) else [],
        compiler_params=_params(2),
        name=name,
    )(*args)


ONES_ROWS = 16


def _vt_kernel(w_ref, x_ref, o_ref, wb_ref):
    @pl.when(pl.program_id(0) == 0)
    def _():
        wb_ref[...] = w_ref[...].astype(BF16)

    vt = lax.dot_general(wb_ref[...], x_ref[...], (((1,), (1,)), ((), ())),
                         preferred_element_type=F32).astype(o_ref.dtype)
    n_tiles, n_heads, _, tv = o_ref.shape
    for t in range(n_tiles):
        for h in range(n_heads):
            o_ref[t, h, :DIFF_DV, :] = vt[h * DIFF_DV:(h + 1) * DIFF_DV, t * tv:(t + 1) * tv]
            o_ref[t, h, DIFF_DV:, :] = jnp.ones((ONES_ROWS, tv), o_ref.dtype)


def _mm_vt(wt, x, *, tv, tn):
    n_out, k = wt.shape
    rows = x.shape[0]
    blk = (tn // tv, DIFF_HEADS, DIFF_DV + ONES_ROWS, tv)
    return pl.pallas_call(
        _vt_kernel,
        grid=(rows // tn,),
        in_specs=[pl.BlockSpec((n_out, k), lambda j: (0, 0)),
                  pl.BlockSpec((tn, k), lambda j: (j, 0))],
        out_specs=pl.BlockSpec(blk, lambda j: (j, 0, 0, 0)),
        out_shape=jax.ShapeDtypeStruct((rows // tv,) + blk[1:], BF16),
        scratch_shapes=[pltpu.VMEM((n_out, k), BF16)],
        compiler_params=_params(1),
        name="v_transposed",
    )(wt, x)


def _merge_kernel(ya_ref, yb_ref, wa_ref, wb_ref, ga_ref, gb_ref, o_ref, wab_ref, wbb_ref):
    pa = jnp.dot(ya_ref[...], _resident_bf16(wa_ref, wab_ref), preferred_element_type=F32)
    pb = jnp.dot(yb_ref[...], _resident_bf16(wb_ref, wbb_ref), preferred_element_type=F32)
    o_ref[...] = (ga_ref[...].astype(F32) * pa + gb_ref[...].astype(F32) * pb).astype(o_ref.dtype)


def _merge(ya, yb, wa, wb, g, *, tm, tn):
    rows, ka = ya.shape
    kb = yb.shape[1]
    n = wa.shape[1]
    nb = n // tn
    return pl.pallas_call(
        _merge_kernel,
        grid=(nb, rows // tm),
        in_specs=[pl.BlockSpec((tm, ka), lambda j, i: (i, 0)),
                  pl.BlockSpec((tm, kb), lambda j, i: (i, 0)),
                  pl.BlockSpec((ka, tn), lambda j, i: (0, j)),
                  pl.BlockSpec((kb, tn), lambda j, i: (0, j)),
                  pl.BlockSpec((tm, tn), lambda j, i: (i, j)),
                  pl.BlockSpec((tm, tn), lambda j, i: (i, nb + j))],
        out_specs=pl.BlockSpec((tm, tn), lambda j, i: (i, j)),
        out_shape=jax.ShapeDtypeStruct((rows, n), BF16),
        scratch_shapes=[pltpu.VMEM((ka, tn), BF16), pltpu.VMEM((kb, tn), BF16)],
        compiler_params=_params(2),
        name="merge",
    )(ya, yb, wa, wb, g, g)


def _resid_kernel(a_ref, w_ref, x_ref, g_ref, o_ref, wb_ref):
    part = jnp.dot(a_ref[...], _resident_bf16(w_ref, wb_ref), preferred_element_type=F32)
    o_ref[...] = x_ref[...] + g_ref[...] * part


def _resid_mm(a, w, xres, mod3, *, gate_chunk, seq, tm, tn, name):
    rows, k = a.shape
    n = w.shape[1]
    per_b = seq // tm
    cpc = n // tn
    return pl.pallas_call(
        _resid_kernel,
        grid=(n // tn, rows // tm),
        in_specs=[pl.BlockSpec((tm, k), lambda j, i: (i, 0)),
                  pl.BlockSpec((k, tn), lambda j, i: (0, j)),
                  pl.BlockSpec((tm, tn), lambda j, i: (i, j)),
                  pl.BlockSpec((None, 1, tn), lambda j, i: (i // per_b, 0, gate_chunk * cpc + j))],
        out_specs=pl.BlockSpec((tm, tn), lambda j, i: (i, j)),
        out_shape=jax.ShapeDtypeStruct((rows, n), F32),
        scratch_shapes=[pltpu.VMEM((k, tn), BF16)],
        compiler_params=_params(2),
        name=name,
    )(a, w, xres, mod3)


def _decay_kernel(h_ref, wa_ref, wup_ref, bup_ref, tri_f_ref, tri_b_ref, o_ref, *, half):
    ga = jnp.dot(h_ref[...], wa_ref[...].astype(BF16), preferred_element_type=F32)
    z = jnp.dot(ga.astype(BF16), wup_ref[...], preferred_element_type=F32) + bup_ref[...]
    la = (jnp.minimum(z, 0.0) - jnp.log1p(jnp.exp(-jnp.abs(z)))) * (1.0 / GLA_TAU)
    hi, lo = _split_bf16(la)
    tt = tri_f_ref.shape[0]
    for tri_ref, sl in ((tri_f_ref, slice(0, half)), (tri_b_ref, slice(half, 2 * half))):
        tri = tri_ref[...]
        for r0 in range(0, hi.shape[0], tt):
            o_ref[r0:r0 + tt, sl] = (
                jnp.dot(tri, hi[r0:r0 + tt, sl], preferred_element_type=F32)
                + jnp.dot(tri, lo[r0:r0 + tt, sl], preferred_element_type=F32))


def _decay(h, wa, wup, bup, *, tm, tt=4 * GLA_CHUNK):
    rows, k = h.shape
    n = wup.shape[1]
    r = np.arange(tt)
    same = (r[:, None] // GLA_CHUNK) == (r[None, :] // GLA_CHUNK)
    tri_f = jnp.asarray(same & (r[None, :] <= r[:, None]), BF16)
    tri_b = jnp.asarray(same & (r[None, :] >= r[:, None]), BF16)
    return pl.pallas_call(
        functools.partial(_decay_kernel, half=n // 2),
        grid=(rows // tm,),
        in_specs=[pl.BlockSpec((tm, k), lambda i: (i, 0)),
                  pl.BlockSpec((k, LANES), lambda i: (0, 0)),
                  pl.BlockSpec((LANES, n), lambda i: (0, 0)),
                  pl.BlockSpec((1, n), lambda i: (0, 0)),
                  pl.BlockSpec((tt, tt), lambda i: (0, 0)),
                  pl.BlockSpec((tt, tt), lambda i: (0, 0))],
        out_specs=pl.BlockSpec((tm, n), lambda i: (i, 0)),
        out_shape=jax.ShapeDtypeStruct((rows, n), F32),
        compiler_params=_params(1),
        name="decay",
    )(h, wa, wup, bup, tri_f, tri_b)


def _gla_operands(q, k, b, *, reverse):
    c = GLA_CHUNK
    bs = c // GLA_REFS
    b_last = b[0:1, :] if reverse else b[c - 1:c, :]

    q_inter = q * jnp.exp(b)
    k_state = (k * jnp.exp(b_last - b)).astype(BF16)

    def rows_padded(t, lo):
        parts = [jnp.zeros((n, LANES), F32) for n in (lo,) if n] + [t]
        parts += [jnp.zeros((n, LANES), F32) for n in (c - lo - t.shape[0],) if n]
        return parts[0] if len(parts) == 1 else jnp.concatenate(parts, axis=0)

    q_segs, k_segs = [], []
    for r in range(GLA_REFS):
        lo, hi = r * bs, (r + 1) * bs
        m_r = b[lo:lo + 1, :] if reverse else b[hi - 1:hi, :]
        k_segs.append(rows_padded(k[lo:hi] * jnp.exp(m_r - b[lo:hi]), lo))
        if reverse:
            q_segs.append(rows_padded(q[:hi] * jnp.exp(b[:hi] - m_r), 0))
        else:
            q_segs.append(rows_padded(q[lo:] * jnp.exp(b[lo:] - m_r), lo))

    def per_head(t):
        ln = lax.broadcasted_iota(jnp.int32, t.shape, 1)
        ma = (ln % LANES) < GLA_DK
        return jnp.concatenate([jnp.where(ma, t, 0.0), jnp.where(ma, 0.0, t)], axis=0).astype(BF16)

    q_stack = per_head(jnp.concatenate(q_segs, axis=1))
    k_stack = jnp.concatenate(k_segs, axis=1).astype(BF16)
    return q_stack, k_stack, per_head(q_inter), k_state, jnp.exp(b_last)


def _gla_group(chunks, state_t, *, reverse):
    c = GLA_CHUNK
    dn_nt = (((1,), (1,)), ((), ()))
    ops = [_gla_operands(q, k, b, reverse=reverse) for q, k, _, b in chunks]
    scores = [lax.dot_general(o[0], o[1], dn_nt, preferred_element_type=F32) for o in ops]
    updates = [jnp.dot(ch[2].T, o[3], preferred_element_type=F32) for ch, o in zip(chunks, ops)]
    row = lax.broadcasted_iota(jnp.int32, (2 * c, c), 0) % c
    col = lax.broadcasted_iota(jnp.int32, (2 * c, c), 1)
    keep = (col >= row) if reverse else (col <= row)
    intra = []
    for ch, s in zip(chunks, scores):
        p = jnp.where(keep, s, 0.0).astype(BF16)
        v = ch[2]
        intra.append((jnp.dot(p[:c], v[:, :GLA_DV], preferred_element_type=F32),
                      jnp.dot(p[c:], v[:, GLA_DV:], preferred_element_type=F32)))
    outs = []
    for o, (ia, ib), upd in zip(ops, intra, updates):
        st = state_t.astype(BF16)
        qi = o[2]
        o_a = ia + lax.dot_general(qi[:c], st[:GLA_DV], dn_nt, preferred_element_type=F32)
        o_b = ib + lax.dot_general(qi[c:], st[GLA_DV:], dn_nt, preferred_element_type=F32)
        outs.append(jnp.concatenate([o_a, o_b], axis=1))
        state_t = o[4] * state_t + upd
    return outs, state_t


def _gla_kernel(*refs, reverse, n_chunks, unroll, write_o, final):
    q_ref, k_ref, v_ref, b_ref, s0_ref = refs[:5]
    pos = 5
    if final:
        of_ref, r_ref, nw_ref = refs[pos:pos + 3]
        pos += 3
    if write_o:
        o_ref = refs[pos]
        pos += 1
    sout_ref, state_ref = refs[pos], refs[pos + 1]
    j = pl.program_id(2)
    c = GLA_CHUNK

    @pl.when(j == 0)
    def _():
        state_ref[...] = s0_ref[...]

    def body(trip, carry):
        starts, chunks = [], []
        for u in range(unroll):
            step = trip * unroll + u
            ci = (n_chunks - 1 - step) if reverse else step
            r0 = pl.multiple_of(ci * c, c)
            starts.append(r0)
            chunks.append((q_ref[pl.ds(r0, c), :].astype(F32) * (GLA_DK ** -0.5),
                           k_ref[pl.ds(r0, c), :].astype(F32),
                           v_ref[pl.ds(r0, c), :],
                           b_ref[pl.ds(r0, c), :]))
        outs, new_state = _gla_group(chunks, state_ref[...], reverse=reverse)
        state_ref[...] = new_state
        for r0, o in zip(starts, outs):
            emit(r0, o)
        return carry

    def emit(r0, o):
        if final:
            o = o + of_ref[pl.ds(r0, c), :]
            gate = r_ref[pl.ds(r0, c), :].astype(F32)
            gate = gate * _sigmoid(gate)
            nw = nw_ref[...]
            ys = []
            for h in range(2):
                oh = o[:, h * GLA_DV:(h + 1) * GLA_DV]
                ms = jnp.mean(oh * oh, axis=-1, keepdims=True)
                ys.append(oh * lax.rsqrt(ms + EPS) * nw)
            o_ref[pl.ds(r0, c), :] = (jnp.concatenate(ys, axis=1) * gate).astype(o_ref.dtype)
        elif write_o:
            o_ref[pl.ds(r0, c), :] = o

    lax.fori_loop(0, n_chunks // unroll, body, 0)

    @pl.when(j == pl.num_programs(2) - 1)
    def _():
        sout_ref[...] = state_ref[...]


def _gla(qkv, bcum, s0, *, row0, seg_len, n_batch, tb, reverse, write_o, o_rows=0, final=None):
    n_pairs = GLA_HEADS // 2
    nblk = seg_len // tb
    assert row0 % tb == 0 and seg_len % tb == 0 and tb % GLA_CHUNK == 0
    qw, vw = LANES, 2 * GLA_DV
    k_col0 = GLA_QK // qw
    v_col0 = (2 * GLA_QK) // vw
    r_col0 = (2 * GLA_QK + GLA_V) // vw
    b_col0 = (GLA_QK // qw) if reverse else 0

    def blk(b, j):
        jj = (nblk - 1 - j) if reverse else j
        return row0 // tb + b * nblk + jj

    def oblk(b, j):
        jj = (nblk - 1 - j) if reverse else j
        return b * nblk + jj

    in_specs = [pl.BlockSpec((tb, qw), lambda b, p, j: (blk(b, j), p)),
                pl.BlockSpec((tb, qw), lambda b, p, j: (blk(b, j), k_col0 + p)),
                pl.BlockSpec((tb, vw), lambda b, p, j: (blk(b, j), v_col0 + p)),
                pl.BlockSpec((tb, qw), lambda b, p, j: (blk(b, j), b_col0 + p)),
                pl.BlockSpec((None, None, vw, qw), lambda b, p, j: (b, p, 0, 0))]
    args = [qkv, qkv, qkv, bcum, s0]
    if final is not None:
        o_fwd, onw = final
        in_specs += [pl.BlockSpec((tb, vw), lambda b, p, j: (oblk(b, j), p)),
                     pl.BlockSpec((tb, vw), lambda b, p, j: (blk(b, j), r_col0 + p)),
                     pl.BlockSpec((1, GLA_DV), lambda b, p, j: (0, 0))]
        args += [o_fwd, qkv, onw.reshape(1, GLA_DV)]
    out_specs, out_shape = [], []
    if write_o:
        out_specs.append(pl.BlockSpec((tb, vw), lambda b, p, j: (oblk(b, j), p)))
        out_shape.append(jax.ShapeDtypeStruct((o_rows, GLA_V), BF16 if final is not None else F32))
    out_specs.append(pl.BlockSpec((None, None, vw, qw), lambda b, p, j: (b, p, 0, 0)))
    out_shape.append(jax.ShapeDtypeStruct(s0.shape, F32))
    outs = pl.pallas_call(
        functools.partial(_gla_kernel, reverse=reverse, n_chunks=tb // GLA_CHUNK,
                          unroll=min(GLA_UNROLL, tb // GLA_CHUNK), write_o=write_o, final=final is not None),
        grid=(n_batch, n_pairs, nblk),
        in_specs=in_specs,
        out_specs=out_specs,
        out_shape=out_shape,
        scratch_shapes=[pltpu.VMEM((vw, qw), F32)],
        compiler_params=_params(3),
        name="gla_" + ("bwd" if reverse else "fwd") + ("_o" if write_o else "_state"),
    )(*args)
    return outs


def _exact_dot(x, m):
    hi, lo = _split_bf16(x)
    return jnp.dot(hi, m, preferred_element_type=F32) + jnp.dot(lo, m, preferred_element_type=F32)


def _qk_prep_kernel(x_ref, w_ref, g_ref, perm_ref, cos_ref, sin_ref, o_ref, *, scale, n_heads):
    g, perm = g_ref[...], perm_ref[...]
    w = w_ref[...]
    cos, sin = cos_ref[...], sin_ref[...]
    for h in range(n_heads):
        sl = slice(h * LANES, (h + 1) * LANES)
        x = x_ref[:, sl].astype(F32)
        y = x * lax.rsqrt(_exact_dot(x * x, g) + EPS) * w
        y = y * cos + _exact_dot(y, perm) * sin
        o_ref[:, sl] = (y * scale).astype(o_ref.dtype)


def _qk_prep(qkv, w, tables, *, col0, rows, tm, scale, table_map):
    n = DIFF_QK
    cos_t, sin_t = tables
    lane = np.arange(LANES)
    gmat = jnp.asarray((lane[:, None] // DIFF_DQK == lane[None, :] // DIFF_DQK) / DIFF_DQK, BF16)
    n_freq = DIFF_DQK // 4
    partner = np.where(lane % (2 * n_freq) < n_freq, lane + n_freq, lane - n_freq)
    perm = jnp.asarray(lane[:, None] == partner[None, :], BF16)
    w2 = jnp.tile(w.reshape(1, DIFF_DQK), (1, LANES // DIFF_DQK))
    tspec = pl.BlockSpec((tm, LANES), lambda i: (table_map(i), 0))
    mspec = pl.BlockSpec((LANES, LANES), lambda i: (0, 0))
    return pl.pallas_call(
        functools.partial(_qk_prep_kernel, scale=scale, n_heads=DIFF_HEADS),
        grid=(rows // tm,),
        in_specs=[pl.BlockSpec((tm, n), lambda i: (i, col0 // n)),
                  pl.BlockSpec((1, LANES), lambda i: (0, 0)),
                  mspec, mspec, tspec, tspec],
        out_specs=pl.BlockSpec((tm, n), lambda i: (i, 0)),
        out_shape=jax.ShapeDtypeStruct((rows, n), BF16),
        compiler_params=_params(1),
        name="qk_prep",
    )(qkv, w2, gmat, perm, cos_t, sin_t)


def _rope_tables(seq, ctx_rows):
    n_freq = DIFF_DQK // 4
    t = np.arange(seq)
    inv_freq = np.float32(ROPE_THETA) ** (-np.arange(n_freq, dtype=np.float32) / np.float32(n_freq))
    ang_row = (t // GRID_W).astype(np.float32)[:, None] * inv_freq
    ang_col = (t % GRID_W).astype(np.float32)[:, None] * inv_freq
    lane = np.arange(LANES)
    use_col = (lane % DIFF_DQK) >= DIFF_DQK // 2
    first_half = (lane % (2 * n_freq)) < n_freq
    ang = np.where(use_col[None, :], ang_col[:, lane % n_freq], ang_row[:, lane % n_freq])
    cos, sin = np.cos(ang).astype(np.float32), np.sin(ang).astype(np.float32)
    signed_sin = np.where(first_half[None, :], -sin, sin)
    pad = lambda a, v: np.concatenate([a, np.full((ctx_rows, LANES), v, np.float32)], axis=0)
    return jnp.asarray(pad(cos, 1.0)), jnp.asarray(pad(signed_sin, 0.0))


def _diff_attn_kernel(q_ref, kc_ref, kl_ref, vc_ref, vl_ref, lam_ref, w_ref, o_ref,
                      qs_ref, p_ref, acc_ref, m_ref, alpha_ref, *, tq, tv, blocks, group):
    k_refs, v_refs = (kc_ref, kl_ref), (vc_ref, vl_ref)
    q = q_ref[...]
    lane = lax.broadcasted_iota(jnp.int32, q.shape, 1)
    zero = jnp.zeros_like(q)
    first = lane < DIFF_DQK
    qs_ref[...] = jnp.concatenate([jnp.where(first, q, zero), jnp.where(first, zero, q)], axis=0)
    width = 2 * tq

    def probabilities(blk, slot):
        seg, k0, nk = blk
        kblk = k_refs[seg][k0:k0 + nk, :]
        for g0 in range(0, width, group):
            cols = slice(g0, g0 + group)
            s = lax.dot_general(kblk, qs_ref[cols, :], (((1,), (1,)), ((), ())),
                                preferred_element_type=F32).astype(BF16)
            m_old = m_ref[:, cols]
            m_new = jnp.maximum(m_old, jnp.max(s, axis=0, keepdims=True).astype(F32))
            m_ref[:, cols] = m_new
            alpha_ref[slot, :, cols] = jnp.exp2(m_old - m_new)
            p_ref[slot, :nk, cols] = jnp.exp2(s - m_new.astype(BF16))

    def values(blk, slot):
        seg, k0, nk = blk
        pv = None
        for t in range(nk // tv):
            part = jnp.dot(v_refs[seg][k0 // tv + t], p_ref[slot, t * tv:(t + 1) * tv, :],
                           preferred_element_type=F32)
            pv = part if pv is None else pv + part
        acc_ref[...] = alpha_ref[slot] * acc_ref[...] + pv

    acc_ref[...] = jnp.zeros_like(acc_ref)
    m_ref[...] = jnp.full(m_ref.shape, -1e30, F32)
    probabilities(blocks[0], 0)
    for n in range(1, len(blocks)):
        probabilities(blocks[n], n % 2)
        values(blocks[n - 1], (n - 1) % 2)
    values(blocks[-1], (len(blocks) - 1) % 2)

    lam_v = lam_ref[...]
    lam = (jnp.exp(jnp.sum(lam_v[0:1] * lam_v[1:2], keepdims=True))
           - jnp.exp(jnp.sum(lam_v[2:3] * lam_v[3:4], keepdims=True)) + LAM_INIT)
    acc = acc_ref[:DIFF_DV, :]
    l = acc_ref[DIFF_DV:DIFF_DV + 1, :]
    o_t = acc[:, :tq] / l[:, :tq] - lam * (acc[:, tq:] / l[:, tq:])
    ms = jnp.mean(o_t * o_t, axis=0, keepdims=True)
    y = o_t * lax.rsqrt(ms + EPS) * w_ref[...] * (1.0 - LAM_INIT)
    o_ref[...] = y.T.astype(o_ref.dtype)


def _diff_attn(qh, kh, vt, lam_vecs, onorm_w, *, n_batch, seq, ctx_len, tv, tq, tk, group):
    dva = DIFF_DV + ONES_ROWS
    assert ctx_len % tv == 0 and tk % tv == 0 and seq % tk == 0 and seq % tq == 0 and ctx_len <= tk
    blocks = ((0, 0, ctx_len),) + tuple((1, j * tk, tk) for j in range(seq // tk))
    nq = seq // tq
    ctx0 = n_batch * seq // ctx_len
    return pl.pallas_call(
        functools.partial(_diff_attn_kernel, tq=tq, tv=tv, blocks=blocks, group=group),
        grid=(n_batch, DIFF_HEADS, nq),
        in_specs=[pl.BlockSpec((tq, LANES), lambda b, h, i: (b * nq + i, h)),
                  pl.BlockSpec((ctx_len, LANES), lambda b, h, i: (ctx0 + b, h)),
                  pl.BlockSpec((seq, LANES), lambda b, h, i: (b, h)),
                  pl.BlockSpec((ctx_len // tv, None, dva, tv), lambda b, h, i: (ctx0 + b, h, 0, 0)),
                  pl.BlockSpec((seq // tv, None, dva, tv), lambda b, h, i: (b, h, 0, 0)),
                  pl.BlockSpec((4, DIFF_DQK), lambda b, h, i: (0, 0)),
                  pl.BlockSpec((DIFF_DV, 1), lambda b, h, i: (0, 0))],
        out_specs=pl.BlockSpec((tq, DIFF_DV), lambda b, h, i: (b * nq + i, h)),
        out_shape=jax.ShapeDtypeStruct((n_batch * seq, DIFF_V), BF16),
        scratch_shapes=[pltpu.VMEM((2 * tq, LANES), BF16),
                        pltpu.VMEM((2, tk, 2 * tq), BF16),
                        pltpu.VMEM((dva, 2 * tq), F32),
                        pltpu.VMEM((1, 2 * tq), F32),
                        pltpu.VMEM((2, 1, 2 * tq), F32)],
        compiler_params=_params(3),
        name="diff_attn",
    )(qh, kh, kh, vt, vt, lam_vecs, onorm_w.reshape(DIFF_DV, 1))


HALO = 8


def _ffn_up_kernel(x_ref, hb_ref, wg_ref, wv_ref, cwg_ref, cwv_ref, cbg_ref, cbv_ref, o_ref,
                   wgb_ref, wvb_ref, ub_ref, ug_ref, uv_ref, *, tm, seq, sub):
    i = pl.program_id(1)
    tn = o_ref.shape[1]

    @pl.when(i == 0)
    def _():
        wgb_ref[...] = wg_ref[...].astype(BF16)
        wvb_ref[...] = wv_ref[...].astype(BF16)
        hb = hb_ref[...]
        ub_ref[:, :tn] = jnp.dot(hb, wgb_ref[...], preferred_element_type=F32)
        ub_ref[:, tn:] = jnp.dot(hb, wvb_ref[...], preferred_element_type=F32)

    t0 = (i * tm) % seq
    first = t0 == 0
    last = t0 + tm == seq
    bnd = ub_ref[pl.ds(pl.multiple_of(i * HALO, HALO), HALO), :]
    streams = ((wgb_ref, ug_ref, cwg_ref, cbg_ref, 0), (wvb_ref, uv_ref, cwv_ref, cbv_ref, tn))
    for _, u_ref, _, _, c0 in streams:
        u_ref[HALO - 1:HALO, :] = jnp.where(first, 0.0, bnd[0:1, c0:c0 + tn])
        u_ref[HALO + tm:HALO + tm + 1, :] = jnp.where(last, 0.0, bnd[1:2, c0:c0 + tn])

    for wb_ref, u_ref, _, _, _ in streams:
        u_ref[HALO:HALO + tm, :] = jnp.dot(x_ref[...], wb_ref[...], preferred_element_type=F32)

    def activate(r):
        lo = HALO + r * sub
        conv = []
        for _, u_ref, cw_ref, cb_ref, _ in streams:
            cw = cw_ref[...]
            win = u_ref[lo - HALO:lo + sub + HALO, :]
            n = win.shape[0]
            before = pltpu.roll(win, 1, axis=0)[HALO:HALO + sub]
            after = pltpu.roll(win, n - 1, axis=0)[HALO:HALO + sub]
            conv.append(before * cw[0:1] + win[HALO:HALO + sub] * cw[1:2] + after * cw[2:3]
                        + cb_ref[...])
        cg, cv = conv
        o_ref[r * sub:(r + 1) * sub, :] = (cg * _sigmoid(cg) * cv).astype(o_ref.dtype)

    for r in range(tm // sub):
        activate(r)


def _ffn_up(h2, w_up, conv_w, conv_b, *, seq, tm, tn, sub):
    rows, k = h2.shape
    dff = w_up.shape[1] // 2
    nj = dff // tn
    nt = rows // tm
    tiles = h2.reshape(nt, tm, k)
    zero = jnp.zeros((1, k), h2.dtype)
    before = jnp.concatenate([zero, tiles[:-1, tm - 1]], axis=0)
    after = jnp.concatenate([tiles[1:, 0], zero], axis=0)
    hb = jnp.concatenate([before[:, None], after[:, None], jnp.zeros((nt, HALO - 2, k), h2.dtype)],
                         axis=1).reshape(nt * HALO, k)
    wsp = lambda off: pl.BlockSpec((k, tn), lambda j, i: (0, off + j))
    csp = lambda off: pl.BlockSpec((CONV_W, tn), lambda j, i: (0, off + j))
    bsp = lambda off: pl.BlockSpec((1, tn), lambda j, i: (0, off + j))
    cb = conv_b.reshape(1, 2 * dff)
    return pl.pallas_call(
        functools.partial(_ffn_up_kernel, tm=tm, seq=seq, sub=sub),
        grid=(nj, nt),
        in_specs=[pl.BlockSpec((tm, k), lambda j, i: (i, 0)),
                  pl.BlockSpec((nt * HALO, k), lambda j, i: (0, 0)),
                  wsp(0), wsp(nj), csp(0), csp(nj), bsp(0), bsp(nj)],
        out_specs=pl.BlockSpec((tm, tn), lambda j, i: (i, j)),
        out_shape=jax.ShapeDtypeStruct((rows, dff), BF16),
        scratch_shapes=[pltpu.VMEM((k, tn), BF16), pltpu.VMEM((k, tn), BF16),
                        pltpu.VMEM((nt * HALO, 2 * tn), F32),
                        pltpu.VMEM((tm + 2 * HALO, tn), F32), pltpu.VMEM((tm + 2 * HALO, tn), F32)],
        compiler_params=_params(2),
        name="ffn_up",
    )(h2, hb, w_up, w_up, conv_w, conv_w, cb, cb)


def kernel(x, c, ctx, c_ctx, w_ada, b_ada, norm1_w, w_in, w_a_up_f, b_a_f, w_a_up_b, b_a_b, gla_onorm_w, diff_qnorm_w, diff_knorm_w, lambda_q1, lambda_k1, lambda_q2, lambda_k2, diff_onorm_w, w_proj_gla, w_proj_diff, w_gate, b_gate, w_out, norm2_w, w_up, conv_w, conv_b, w_down):
    n_batch, seq, d = x.shape
    ctx_len = ctx.shape[1]
    assert w_ada.shape[0] == 1, "single-layer block"
    rows = n_batch * seq
    crow = n_batch * ctx_len

    cs = jnp.concatenate([c, c_ctx[None, :], jnp.zeros((8 - n_batch - 1, d), F32)], axis=0)
    mod = _ada(cs, w_ada[0], b_ada[0])
    mod3 = mod.reshape(8, 1, 6 * d)

    o_r = 2 * GLA_QK + 2 * GLA_V
    o_d = o_r + 2 * GLA_RANK
    o_v = o_d + 2 * DIFF_QK
    w_i = w_in[0]
    w_dv_t = w_i[:, o_v:].T
    w_lr = jnp.pad(w_i[:, o_r:o_d], ((0, 0), (0, LANES - 2 * GLA_RANK)))
    w_upcat = jnp.zeros((LANES, 2 * GLA_QK), F32)
    w_upcat = w_upcat.at[:GLA_RANK, :GLA_QK].set(w_a_up_f[0])
    w_upcat = w_upcat.at[GLA_RANK:2 * GLA_RANK, GLA_QK:].set(w_a_up_b[0]).astype(BF16)
    b_upcat = jnp.concatenate([b_a_f[0], b_a_b[0]]).reshape(1, 2 * GLA_QK)

    x2 = x.reshape(rows, d)
    t_norm = TILES["norm"]
    h_all = _norm1(x2, ctx.reshape(crow, d), norm1_w[0], mod3, seq=seq, n_batch=n_batch, tm=t_norm)
    all_rows = rows + crow
    in_tiles = dict(rows=all_rows, tm=all_rows // 8, **TILES["in_proj"])
    qkv = _mm(h_all, w_i, n_cols=o_r, name="in_proj_gla", **in_tiles)
    dqk = _mm(h_all, w_i, n_cols=2 * DIFF_QK, name="in_proj_diff", col0=o_r, shift=o_d - o_r,
              **in_tiles)
    v_t = _mm_vt(w_dv_t, h_all, tv=ctx_len, tn=t_norm)
    bcum = _decay(h_all, w_lr, w_upcat, b_upcat, tm=t_norm)

    s_zero = jnp.zeros((n_batch, GLA_HEADS // 2, 2 * GLA_DV, LANES), F32)
    seg_c = dict(row0=rows, seg_len=ctx_len, n_batch=n_batch, tb=ctx_len)
    seg_l = dict(row0=0, seg_len=seq, n_batch=n_batch, tb=TILES["gla_rows"])
    (s_cf,) = _gla(qkv, bcum, s_zero, reverse=False, write_o=False, **seg_c)
    (s_cb,) = _gla(qkv, bcum, s_zero, reverse=True, write_o=False, **seg_c)
    o_lf, _ = _gla(qkv, bcum, s_cf, reverse=False, write_o=True, o_rows=rows, **seg_l)
    y_a, _ = _gla(qkv, bcum, s_cb, reverse=True, write_o=True, o_rows=rows,
                  final=(o_lf, gla_onorm_w[0]), **seg_l)

    t_prep = t_norm
    tables = _rope_tables(seq, t_prep)
    lat_tiles = seq // t_prep
    qh = _qk_prep(dqk, diff_qnorm_w[0], tables, col0=0, rows=rows, tm=t_prep,
                  scale=(DIFF_DQK ** -0.5) * LOG2E, table_map=lambda i: i % lat_tiles)
    kh = _qk_prep(dqk, diff_knorm_w[0], tables, col0=DIFF_QK, rows=all_rows, tm=t_prep, scale=1.0,
                  table_map=lambda i: jnp.where(i < n_batch * lat_tiles, i % lat_tiles, lat_tiles))
    lam_vecs = jnp.stack([lambda_q1[0], lambda_k1[0], lambda_q2[0], lambda_k2[0]], axis=0)
    y_b = _diff_attn(qh, kh, v_t, lam_vecs, diff_onorm_w[0], n_batch=n_batch, seq=seq, ctx_len=ctx_len,
                     tv=ctx_len, **TILES["attention"])

    dense = TILES["dense"]
    g = _mm(h_all, w_gate[0], rows=rows, n_cols=2 * d, name="gate", bias=b_gate[0], **dense)
    merged = _merge(y_a, y_b, w_proj_gla[0], w_proj_diff[0], g, **dense)
    x1 = _resid_mm(merged, w_out[0], x2, mod3, gate_chunk=2, seq=seq, name="out_proj", **dense)

    h2 = _norm2(x1, norm2_w[0], mod3, seq=seq, tm=t_norm)
    act = _ffn_up(h2, w_up[0], conv_w[0], conv_b[0], seq=seq, **TILES["ffn_up"])
    out = _resid_mm(act, w_down[0], x1, mod3, gate_chunk=5, seq=seq, name="ffn_down",
                    **TILES["ffn_down"])
    return out.reshape(n_batch, seq, d)
```

```python
import functools
import math

import jax
import jax.numpy as jnp
import numpy as np
from jax import lax
from jax.experimental import pallas as pl
from jax.experimental.pallas import tpu as pltpu

F32 = jnp.float32
BF16 = jnp.bfloat16

GRID_W = 64
GLA_HEADS = 8
GLA_DK = 64
GLA_DV = 128
GLA_RANK = 16
GLA_TAU = 16.0
DIFF_HEADS = 8
DIFF_DQK = 64
DIFF_DV = 128
CONV_W = 3
ROPE_THETA = 10000.0
EPS = 1e-6
LAM_INIT = 0.8 - 0.6 * math.exp(-0.3 * 0)

GLA_QK = GLA_HEADS * GLA_DK
GLA_V = GLA_HEADS * GLA_DV
DIFF_QK = DIFF_HEADS * 2 * DIFF_DQK
DIFF_V = DIFF_HEADS * DIFF_DV

LANES = 128
V7X_VMEM_BYTES = 64 * 1024 * 1024
VMEM_LIMIT = V7X_VMEM_BYTES - 8 * 1024 * 1024

GLA_CHUNK = 64
GLA_REFS = 4
GLA_UNROLL = 16
LOG2E = 1.4426950408889634

TILES = dict(
    norm=512,
    dense=dict(tm=1024, tn=1024),
    in_proj=dict(tn=1024),
    ffn_up=dict(tm=1024, tn=512, sub=256),
    ffn_down=dict(tm=512, tn=512),
    gla_rows=2048,
    attention=dict(tq=512, tk=256, group=256),
)


def _params(n_axes):
    return pltpu.CompilerParams(dimension_semantics=("arbitrary",) * n_axes,
                                vmem_limit_bytes=VMEM_LIMIT)


def _sigmoid(x):
    return 1.0 / (1.0 + jnp.exp2(x * (-LOG2E)))


def _split_bf16(x):
    hi = x.astype(BF16)
    lo = (x - hi.astype(F32)).astype(BF16)
    return hi, lo


def _ada_kernel(c_ref, w_ref, b_ref, o_ref):
    c = c_ref[...]
    s = (c * _sigmoid(c)).astype(BF16)
    o_ref[...] = jnp.dot(s, w_ref[...].astype(BF16), preferred_element_type=F32) + b_ref[...]


def _ada(cs, w, b):
    d, n = w.shape
    tn = TILES["dense"]["tn"]
    return pl.pallas_call(
        _ada_kernel,
        grid=(n // tn,),
        in_specs=[pl.BlockSpec((8, d), lambda j: (0, 0)),
                  pl.BlockSpec((d, tn), lambda j: (0, j)),
                  pl.BlockSpec((1, tn), lambda j: (0, j))],
        out_specs=pl.BlockSpec((8, tn), lambda j: (0, j)),
        out_shape=jax.ShapeDtypeStruct((8, n), F32),
        compiler_params=_params(1),
        name="ada",
    )(cs, w, b.reshape(1, n))


NORM_ROWS = 16


def _norm_mod(x_ref, nw_ref, sh_ref, sc_ref, o_ref):
    gain = nw_ref[...] * (1.0 + sc_ref[...])
    shift = sh_ref[...]

    def body(r, carry):
        rows = pl.ds(pl.multiple_of(r * NORM_ROWS, NORM_ROWS), NORM_ROWS)
        x = x_ref[rows, :]
        ms = jnp.mean(x * x, axis=-1, keepdims=True)
        o_ref[rows, :] = (x * lax.rsqrt(ms + EPS) * gain + shift).astype(o_ref.dtype)
        return carry

    lax.fori_loop(0, x_ref.shape[0] // NORM_ROWS, body, 0, unroll=4)


def _norm1_kernel(x_ref, ctx_ref, nw_ref, sh_ref, sc_ref, o_ref, *, n_lat_tiles):
    i = pl.program_id(0)

    @pl.when(i < n_lat_tiles)
    def _():
        _norm_mod(x_ref, nw_ref, sh_ref, sc_ref, o_ref)

    @pl.when(i >= n_lat_tiles)
    def _():
        _norm_mod(ctx_ref, nw_ref, sh_ref, sc_ref, o_ref)


def _norm1(x2, ctx2, nw, mod3, *, seq, n_batch, tm):
    rows, d = x2.shape
    crow = ctx2.shape[0]
    assert crow == tm and rows % tm == 0 and seq % tm == 0
    nl = rows // tm
    per_b = seq // tm

    def mod_row(i):
        return jnp.where(i < nl, i // per_b, n_batch)

    return pl.pallas_call(
        functools.partial(_norm1_kernel, n_lat_tiles=nl),
        grid=(nl + 1,),
        in_specs=[pl.BlockSpec((tm, d), lambda i: (jnp.minimum(i, nl - 1), 0)),
                  pl.BlockSpec((tm, d), lambda i: (0, 0)),
                  pl.BlockSpec((1, d), lambda i: (0, 0)),
                  pl.BlockSpec((None, 1, d), lambda i: (mod_row(i), 0, 0)),
                  pl.BlockSpec((None, 1, d), lambda i: (mod_row(i), 0, 1))],
        out_specs=pl.BlockSpec((tm, d), lambda i: (i, 0)),
        out_shape=jax.ShapeDtypeStruct((rows + crow, d), BF16),
        compiler_params=_params(1),
        name="norm1",
    )(x2, ctx2, nw.reshape(1, d), mod3, mod3)


def _norm2_kernel(x_ref, nw_ref, sh_ref, sc_ref, o_ref):
    _norm_mod(x_ref, nw_ref, sh_ref, sc_ref, o_ref)


def _norm2(x2, nw, mod3, *, seq, tm):
    rows, d = x2.shape
    per_b = seq // tm
    return pl.pallas_call(
        _norm2_kernel,
        grid=(rows // tm,),
        in_specs=[pl.BlockSpec((tm, d), lambda i: (i, 0)),
                  pl.BlockSpec((1, d), lambda i: (0, 0)),
                  pl.BlockSpec((None, 1, d), lambda i: (i // per_b, 0, 3)),
                  pl.BlockSpec((None, 1, d), lambda i: (i // per_b, 0, 4))],
        out_specs=pl.BlockSpec((tm, d), lambda i: (i, 0)),
        out_shape=jax.ShapeDtypeStruct((rows, d), BF16),
        compiler_params=_params(1),
        name="norm2",
    )(x2, nw.reshape(1, d), mod3, mod3)


def _resident_bf16(w_ref, wb_ref):
    @pl.when(pl.program_id(1) == 0)
    def _():
        wb_ref[...] = w_ref[...].astype(BF16)
    return wb_ref[...]


def _mm_kernel(x_ref, w_ref, *rest, gate, shift):
    if shift:
        tail_ref, *rest = rest
    if gate:
        b_ref, o_ref, *scratch = rest
    else:
        o_ref, *scratch = rest
    if shift:
        @pl.when(pl.program_id(1) == 0)
        def _():
            wide = jnp.concatenate([w_ref[...], tail_ref[...]], axis=1)
            scratch[0][...] = wide[:, shift:shift + w_ref.shape[1]].astype(BF16)
        w = scratch[0][...]
    else:
        w = _resident_bf16(w_ref, scratch[0]) if scratch else w_ref[...]
    z = jnp.dot(x_ref[...], w, preferred_element_type=F32)
    if gate:
        z = _sigmoid(z + b_ref[...])
    o_ref[...] = z.astype(o_ref.dtype)


def _mm(x, w, *, rows, tm, tn, n_cols, name, bias=None, col0=0, shift=0):
    k = w.shape[0]
    gate = bias is not None
    assert col0 % tn == 0 and 0 <= shift < LANES
    in_specs = [pl.BlockSpec((tm, k), lambda j, i: (i, 0)),
                pl.BlockSpec((k, tn), lambda j, i: (0, col0 // tn + j))]
    args = [x, w]
    if shift:
        in_specs.append(pl.BlockSpec((k, LANES), lambda j, i: (0, (col0 + (j + 1) * tn) // LANES)))
        args.append(w)
    if gate:
        in_specs.append(pl.BlockSpec((1, tn), lambda j, i: (0, j)))
        args.append(bias.reshape(1, -1))
    return pl.pallas_call(
        functools.partial(_mm_kernel, gate=gate, shift=shift),
        grid=(n_cols // tn, rows // tm),
        in_specs=in_specs,
        out_specs=pl.BlockSpec((tm, tn), lambda j, i: (i, j)),
        out_shape=jax.ShapeDtypeStruct((rows, n_cols), BF16),
        scratch_shapes=[pltpu.VMEM((k, tn), BF16)] if (w.dtype != BF16 or shift) else [],
        compiler_params=_params(2),
        name=name,
    )(*args)


def _vt_kernel(w_ref, x_ref, o_ref, wb_ref):
    @pl.when(pl.program_id(0) == 0)
    def _():
        wb_ref[...] = w_ref[...].astype(BF16)

    vt = lax.dot_general(wb_ref[...], x_ref[...], (((1,), (1,)), ((), ())),
                         preferred_element_type=F32).astype(o_ref.dtype)
    n_tiles, n_heads, _, tv = o_ref.shape
    for t in range(n_tiles):
        for h in range(n_heads):
            o_ref[t, h] = vt[h * DIFF_DV:(h + 1) * DIFF_DV, t * tv:(t + 1) * tv]


def _mm_vt(wt, x, *, tv, tn):
    n_out, k = wt.shape
    rows = x.shape[0]
    blk = (tn // tv, DIFF_HEADS, DIFF_DV, tv)
    return pl.pallas_call(
        _vt_kernel,
        grid=(rows // tn,),
        in_specs=[pl.BlockSpec((n_out, k), lambda j: (0, 0)),
                  pl.BlockSpec((tn, k), lambda j: (j, 0))],
        out_specs=pl.BlockSpec(blk, lambda j: (j, 0, 0, 0)),
        out_shape=jax.ShapeDtypeStruct((rows // tv,) + blk[1:], BF16),
        scratch_shapes=[pltpu.VMEM((n_out, k), BF16)],
        compiler_params=_params(1),
        name="v_transposed",
    )(wt, x)


def _merge_kernel(ya_ref, yb_ref, wa_ref, wb_ref, ga_ref, gb_ref, o_ref, wab_ref, wbb_ref):
    pa = jnp.dot(ya_ref[...], _resident_bf16(wa_ref, wab_ref), preferred_element_type=F32)
    pb = jnp.dot(yb_ref[...], _resident_bf16(wb_ref, wbb_ref), preferred_element_type=F32)
    o_ref[...] = (ga_ref[...].astype(F32) * pa + gb_ref[...].astype(F32) * pb).astype(o_ref.dtype)


def _merge(ya, yb, wa, wb, g, *, tm, tn):
    rows, ka = ya.shape
    kb = yb.shape[1]
    n = wa.shape[1]
    nb = n // tn
    return pl.pallas_call(
        _merge_kernel,
        grid=(nb, rows // tm),
        in_specs=[pl.BlockSpec((tm, ka), lambda j, i: (i, 0)),
                  pl.BlockSpec((tm, kb), lambda j, i: (i, 0)),
                  pl.BlockSpec((ka, tn), lambda j, i: (0, j)),
                  pl.BlockSpec((kb, tn), lambda j, i: (0, j)),
                  pl.BlockSpec((tm, tn), lambda j, i: (i, j)),
                  pl.BlockSpec((tm, tn), lambda j, i: (i, nb + j))],
        out_specs=pl.BlockSpec((tm, tn), lambda j, i: (i, j)),
        out_shape=jax.ShapeDtypeStruct((rows, n), BF16),
        scratch_shapes=[pltpu.VMEM((ka, tn), BF16), pltpu.VMEM((kb, tn), BF16)],
        compiler_params=_params(2),
        name="merge",
    )(ya, yb, wa, wb, g, g)


def _resid_kernel(a_ref, w_ref, x_ref, g_ref, o_ref, wb_ref):
    part = jnp.dot(a_ref[...], _resident_bf16(w_ref, wb_ref), preferred_element_type=F32)
    o_ref[...] = x_ref[...] + g_ref[...] * part


def _resid_mm(a, w, xres, mod3, *, gate_chunk, seq, tm, tn, name):
    rows, k = a.shape
    n = w.shape[1]
    per_b = seq // tm
    cpc = n // tn
    return pl.pallas_call(
        _resid_kernel,
        grid=(n // tn, rows // tm),
        in_specs=[pl.BlockSpec((tm, k), lambda j, i: (i, 0)),
                  pl.BlockSpec((k, tn), lambda j, i: (0, j)),
                  pl.BlockSpec((tm, tn), lambda j, i: (i, j)),
                  pl.BlockSpec((None, 1, tn), lambda j, i: (i // per_b, 0, gate_chunk * cpc + j))],
        out_specs=pl.BlockSpec((tm, tn), lambda j, i: (i, j)),
        out_shape=jax.ShapeDtypeStruct((rows, n), F32),
        scratch_shapes=[pltpu.VMEM((k, tn), BF16)],
        compiler_params=_params(2),
        name=name,
    )(a, w, xres, mod3)


def _decay_kernel(h_ref, wa_ref, wup_ref, bup_ref, tri_f_ref, tri_b_ref, o_ref, *, half):
    ga = jnp.dot(h_ref[...], wa_ref[...].astype(BF16), preferred_element_type=F32)
    z = jnp.dot(ga.astype(BF16), wup_ref[...], preferred_element_type=F32) + bup_ref[...]
    la = (jnp.minimum(z, 0.0) - jnp.log(1.0 + jnp.exp(-jnp.abs(z)))) * (1.0 / GLA_TAU)
    hi, lo = _split_bf16(la)
    tt = tri_f_ref.shape[0]
    for tri_ref, sl in ((tri_f_ref, slice(0, half)), (tri_b_ref, slice(half, 2 * half))):
        tri = tri_ref[...]
        for r0 in range(0, hi.shape[0], tt):
            o_ref[r0:r0 + tt, sl] = (
                jnp.dot(tri, hi[r0:r0 + tt, sl], preferred_element_type=F32)
                + jnp.dot(tri, lo[r0:r0 + tt, sl], preferred_element_type=F32))


def _decay(h, wa, wup, bup, *, tm, tt=4 * GLA_CHUNK):
    rows, k = h.shape
    n = wup.shape[1]
    r = np.arange(tt)
    same = (r[:, None] // GLA_CHUNK) == (r[None, :] // GLA_CHUNK)
    tri_f = jnp.asarray(same & (r[None, :] <= r[:, None]), BF16)
    tri_b = jnp.asarray(same & (r[None, :] >= r[:, None]), BF16)
    return pl.pallas_call(
        functools.partial(_decay_kernel, half=n // 2),
        grid=(rows // tm,),
        in_specs=[pl.BlockSpec((tm, k), lambda i: (i, 0)),
                  pl.BlockSpec((k, LANES), lambda i: (0, 0)),
                  pl.BlockSpec((LANES, n), lambda i: (0, 0)),
                  pl.BlockSpec((1, n), lambda i: (0, 0)),
                  pl.BlockSpec((tt, tt), lambda i: (0, 0)),
                  pl.BlockSpec((tt, tt), lambda i: (0, 0))],
        out_specs=pl.BlockSpec((tm, n), lambda i: (i, 0)),
        out_shape=jax.ShapeDtypeStruct((rows, n), F32),
        compiler_params=_params(1),
        name="decay",
    )(h, wa, wup, bup, tri_f, tri_b)


def _gla_operands(q, k, b, *, reverse):
    c = GLA_CHUNK
    bs = c // GLA_REFS
    b_last = b[0:1, :] if reverse else b[c - 1:c, :]

    q_inter = q * jnp.exp(b)
    k_state = (k * jnp.exp(b_last - b)).astype(BF16)

    def rows_padded(t, lo):
        parts = [jnp.zeros((n, LANES), F32) for n in (lo,) if n] + [t]
        parts += [jnp.zeros((n, LANES), F32) for n in (c - lo - t.shape[0],) if n]
        return parts[0] if len(parts) == 1 else jnp.concatenate(parts, axis=0)

    q_segs, k_segs = [], []
    for r in range(GLA_REFS):
        lo, hi = r * bs, (r + 1) * bs
        m_r = b[lo:lo + 1, :] if reverse else b[hi - 1:hi, :]
        k_segs.append(rows_padded(k[lo:hi] * jnp.exp(m_r - b[lo:hi]), lo))
        if reverse:
            q_segs.append(rows_padded(q[:hi] * jnp.exp(b[:hi] - m_r), 0))
        else:
            q_segs.append(rows_padded(q[lo:] * jnp.exp(b[lo:] - m_r), lo))

    def per_head(t):
        ln = lax.broadcasted_iota(jnp.int32, t.shape, 1)
        ma = (ln % LANES) < GLA_DK
        return jnp.concatenate([jnp.where(ma, t, 0.0), jnp.where(ma, 0.0, t)], axis=0).astype(BF16)

    q_stack = per_head(jnp.concatenate(q_segs, axis=1))
    k_stack = jnp.concatenate(k_segs, axis=1).astype(BF16)
    return q_stack, k_stack, per_head(q_inter), k_state, jnp.exp(b_last)


def _gla_group(chunks, state_t, *, reverse):
    c = GLA_CHUNK
    dn_nt = (((1,), (1,)), ((), ()))
    ops = [_gla_operands(q, k, b, reverse=reverse) for q, k, _, b in chunks]
    scores = [lax.dot_general(o[0], o[1], dn_nt, preferred_element_type=F32) for o in ops]
    updates = [jnp.dot(ch[2].T, o[3], preferred_element_type=F32) for ch, o in zip(chunks, ops)]
    row = lax.broadcasted_iota(jnp.int32, (2 * c, c), 0) % c
    col = lax.broadcasted_iota(jnp.int32, (2 * c, c), 1)
    keep = (col >= row) if reverse else (col <= row)
    intra = []
    for ch, s in zip(chunks, scores):
        p = jnp.where(keep, s, 0.0).astype(BF16)
        v = ch[2]
        intra.append((jnp.dot(p[:c], v[:, :GLA_DV], preferred_element_type=F32),
                      jnp.dot(p[c:], v[:, GLA_DV:], preferred_element_type=F32)))
    outs = []
    for o, (ia, ib), upd in zip(ops, intra, updates):
        st = state_t.astype(BF16)
        qi = o[2]
        o_a = ia + lax.dot_general(qi[:c], st[:GLA_DV], dn_nt, preferred_element_type=F32)
        o_b = ib + lax.dot_general(qi[c:], st[GLA_DV:], dn_nt, preferred_element_type=F32)
        outs.append(jnp.concatenate([o_a, o_b], axis=1))
        state_t = o[4] * state_t + upd
    return outs, state_t


def _gla_kernel(*refs, reverse, n_chunks, unroll, write_o, final):
    q_ref, k_ref, v_ref, b_ref, s0_ref = refs[:5]
    pos = 5
    if final:
        of_ref, r_ref, nw_ref = refs[pos:pos + 3]
        pos += 3
    if write_o:
        o_ref = refs[pos]
        pos += 1
    sout_ref, state_ref = refs[pos], refs[pos + 1]
    j = pl.program_id(2)
    c = GLA_CHUNK

    @pl.when(j == 0)
    def _():
        state_ref[...] = s0_ref[...]

    def body(trip, carry):
        starts, chunks = [], []
        for u in range(unroll):
            step = trip * unroll + u
            ci = (n_chunks - 1 - step) if reverse else step
            r0 = pl.multiple_of(ci * c, c)
            starts.append(r0)
            chunks.append((q_ref[pl.ds(r0, c), :].astype(F32) * (GLA_DK ** -0.5),
                           k_ref[pl.ds(r0, c), :].astype(F32),
                           v_ref[pl.ds(r0, c), :],
                           b_ref[pl.ds(r0, c), :]))
        outs, new_state = _gla_group(chunks, state_ref[...], reverse=reverse)
        state_ref[...] = new_state
        for r0, o in zip(starts, outs):
            emit(r0, o)
        return carry

    def emit(r0, o):
        if final:
            o = o + of_ref[pl.ds(r0, c), :]
            gate = r_ref[pl.ds(r0, c), :].astype(F32)
            gate = gate * _sigmoid(gate)
            nw = nw_ref[...]
            ys = []
            for h in range(2):
                oh = o[:, h * GLA_DV:(h + 1) * GLA_DV]
                ms = jnp.mean(oh * oh, axis=-1, keepdims=True)
                ys.append(oh * lax.rsqrt(ms + EPS) * nw)
            o_ref[pl.ds(r0, c), :] = (jnp.concatenate(ys, axis=1) * gate).astype(o_ref.dtype)
        elif write_o:
            o_ref[pl.ds(r0, c), :] = o

    lax.fori_loop(0, n_chunks // unroll, body, 0)

    @pl.when(j == pl.num_programs(2) - 1)
    def _():
        sout_ref[...] = state_ref[...]


def _gla(qkv, bcum, s0, *, row0, seg_len, n_batch, tb, reverse, write_o, o_rows=0, final=None):
    n_pairs = GLA_HEADS // 2
    nblk = seg_len // tb
    assert row0 % tb == 0 and seg_len % tb == 0 and tb % GLA_CHUNK == 0
    qw, vw = LANES, 2 * GLA_DV
    k_col0 = GLA_QK // qw
    v_col0 = (2 * GLA_QK) // vw
    r_col0 = (2 * GLA_QK + GLA_V) // vw
    b_col0 = (GLA_QK // qw) if reverse else 0

    def blk(b, j):
        jj = (nblk - 1 - j) if reverse else j
        return row0 // tb + b * nblk + jj

    def oblk(b, j):
        jj = (nblk - 1 - j) if reverse else j
        return b * nblk + jj

    in_specs = [pl.BlockSpec((tb, qw), lambda b, p, j: (blk(b, j), p)),
                pl.BlockSpec((tb, qw), lambda b, p, j: (blk(b, j), k_col0 + p)),
                pl.BlockSpec((tb, vw), lambda b, p, j: (blk(b, j), v_col0 + p)),
                pl.BlockSpec((tb, qw), lambda b, p, j: (blk(b, j), b_col0 + p)),
                pl.BlockSpec((None, None, vw, qw), lambda b, p, j: (b, p, 0, 0))]
    args = [qkv, qkv, qkv, bcum, s0]
    if final is not None:
        o_fwd, onw = final
        in_specs += [pl.BlockSpec((tb, vw), lambda b, p, j: (oblk(b, j), p)),
                     pl.BlockSpec((tb, vw), lambda b, p, j: (blk(b, j), r_col0 + p)),
                     pl.BlockSpec((1, GLA_DV), lambda b, p, j: (0, 0))]
        args += [o_fwd, qkv, onw.reshape(1, GLA_DV)]
    out_specs, out_shape = [], []
    if write_o:
        out_specs.append(pl.BlockSpec((tb, vw), lambda b, p, j: (oblk(b, j), p)))
        out_shape.append(jax.ShapeDtypeStruct((o_rows, GLA_V), BF16 if final is not None else F32))
    out_specs.append(pl.BlockSpec((None, None, vw, qw), lambda b, p, j: (b, p, 0, 0)))
    out_shape.append(jax.ShapeDtypeStruct(s0.shape, F32))
    outs = pl.pallas_call(
        functools.partial(_gla_kernel, reverse=reverse, n_chunks=tb // GLA_CHUNK,
                          unroll=min(GLA_UNROLL, tb // GLA_CHUNK), write_o=write_o, final=final is not None),
        grid=(n_batch, n_pairs, nblk),
        in_specs=in_specs,
        out_specs=out_specs,
        out_shape=out_shape,
        scratch_shapes=[pltpu.VMEM((vw, qw), F32)],
        compiler_params=_params(3),
        name="gla_" + ("bwd" if reverse else "fwd") + ("_o" if write_o else "_state"),
    )(*args)
    return outs


def _exact_dot(x, m):
    hi, lo = _split_bf16(x)
    return jnp.dot(hi, m, preferred_element_type=F32) + jnp.dot(lo, m, preferred_element_type=F32)


def _qk_prep_kernel(x_ref, w_ref, g_ref, perm_ref, cos_ref, sin_ref, o_ref, *, scale, n_heads):
    g, perm = g_ref[...], perm_ref[...]
    w = w_ref[...]
    cos, sin = cos_ref[...], sin_ref[...]
    for h in range(n_heads):
        sl = slice(h * LANES, (h + 1) * LANES)
        x = x_ref[:, sl].astype(F32)
        y = x * lax.rsqrt(_exact_dot(x * x, g) + EPS) * w
        y = y * cos + _exact_dot(y, perm) * sin
        o_ref[:, sl] = (y * scale).astype(o_ref.dtype)


def _qk_prep(qkv, w, tables, *, col0, rows, tm, scale, table_map):
    n = DIFF_QK
    cos_t, sin_t = tables
    lane = np.arange(LANES)
    gmat = jnp.asarray((lane[:, None] // DIFF_DQK == lane[None, :] // DIFF_DQK) / DIFF_DQK, BF16)
    n_freq = DIFF_DQK // 4
    partner = np.where(lane % (2 * n_freq) < n_freq, lane + n_freq, lane - n_freq)
    perm = jnp.asarray(lane[:, None] == partner[None, :], BF16)
    w2 = jnp.tile(w.reshape(1, DIFF_DQK), (1, LANES // DIFF_DQK))
    tspec = pl.BlockSpec((tm, LANES), lambda i: (table_map(i), 0))
    mspec = pl.BlockSpec((LANES, LANES), lambda i: (0, 0))
    return pl.pallas_call(
        functools.partial(_qk_prep_kernel, scale=scale, n_heads=DIFF_HEADS),
        grid=(rows // tm,),
        in_specs=[pl.BlockSpec((tm, n), lambda i: (i, col0 // n)),
                  pl.BlockSpec((1, LANES), lambda i: (0, 0)),
                  mspec, mspec, tspec, tspec],
        out_specs=pl.BlockSpec((tm, n), lambda i: (i, 0)),
        out_shape=jax.ShapeDtypeStruct((rows, n), BF16),
        compiler_params=_params(1),
        name="qk_prep",
    )(qkv, w2, gmat, perm, cos_t, sin_t)


def _rope_tables(seq, ctx_rows):
    n_freq = DIFF_DQK // 4
    t = np.arange(seq)
    inv_freq = np.float32(ROPE_THETA) ** (-np.arange(n_freq, dtype=np.float32) / np.float32(n_freq))
    ang_row = (t // GRID_W).astype(np.float32)[:, None] * inv_freq
    ang_col = (t % GRID_W).astype(np.float32)[:, None] * inv_freq
    lane = np.arange(LANES)
    use_col = (lane % DIFF_DQK) >= DIFF_DQK // 2
    first_half = (lane % (2 * n_freq)) < n_freq
    ang = np.where(use_col[None, :], ang_col[:, lane % n_freq], ang_row[:, lane % n_freq])
    cos, sin = np.cos(ang).astype(np.float32), np.sin(ang).astype(np.float32)
    signed_sin = np.where(first_half[None, :], -sin, sin)
    pad = lambda a, v: np.concatenate([a, np.full((ctx_rows, LANES), v, np.float32)], axis=0)
    return jnp.asarray(pad(cos, 1.0)), jnp.asarray(pad(signed_sin, 0.0))


def _diff_attn_kernel(q_ref, kc_ref, kl_ref, vc_ref, vl_ref, lam_ref, w_ref, o_ref,
                      qs_ref, p_ref, acc_ref, m_ref, alpha_ref, l_ref, *, tq, tv, blocks, group):
    k_refs, v_refs = (kc_ref, kl_ref), (vc_ref, vl_ref)
    q = q_ref[...]
    lane = lax.broadcasted_iota(jnp.int32, q.shape, 1)
    zero = jnp.zeros_like(q)
    first = lane < DIFF_DQK
    qs_ref[...] = jnp.concatenate([jnp.where(first, q, zero), jnp.where(first, zero, q)], axis=0)
    width = 2 * tq

    def probabilities(blk, slot):
        seg, k0, nk = blk
        kblk = k_refs[seg][k0:k0 + nk, :]
        for g0 in range(0, width, group):
            cols = slice(g0, g0 + group)
            s = lax.dot_general(kblk, qs_ref[cols, :], (((1,), (1,)), ((), ())),
                                preferred_element_type=F32).astype(BF16)
            m_old = m_ref[:, cols]
            m_new = jnp.maximum(m_old, jnp.max(s, axis=0, keepdims=True).astype(F32))
            m_ref[:, cols] = m_new
            alpha = jnp.exp2(m_old - m_new)
            alpha_ref[slot, :, cols] = alpha
            p = jnp.exp2(s - m_new.astype(BF16))
            p_ref[slot, :nk, cols] = p
            l_ref[:, cols] = alpha * l_ref[:, cols] + jnp.sum(p.astype(F32), axis=0, keepdims=True)

    def values(blk, slot):
        seg, k0, nk = blk
        pv = None
        for t in range(nk // tv):
            part = jnp.dot(v_refs[seg][k0 // tv + t], p_ref[slot, t * tv:(t + 1) * tv, :],
                           preferred_element_type=F32)
            pv = part if pv is None else pv + part
        acc_ref[...] = alpha_ref[slot] * acc_ref[...] + pv

    acc_ref[...] = jnp.zeros_like(acc_ref)
    l_ref[...] = jnp.zeros_like(l_ref)
    m_ref[...] = jnp.full(m_ref.shape, -1e30, F32)
    probabilities(blocks[0], 0)
    for n in range(1, len(blocks)):
        probabilities(blocks[n], n % 2)
        values(blocks[n - 1], (n - 1) % 2)
    values(blocks[-1], (len(blocks) - 1) % 2)

    lam_v = lam_ref[...]
    lam = (jnp.exp(jnp.sum(lam_v[0:1] * lam_v[1:2], keepdims=True))
           - jnp.exp(jnp.sum(lam_v[2:3] * lam_v[3:4], keepdims=True)) + LAM_INIT)
    acc = acc_ref[...]
    l = l_ref[...]
    o_t = acc[:, :tq] / l[:, :tq] - lam * (acc[:, tq:] / l[:, tq:])
    ms = jnp.mean(o_t * o_t, axis=0, keepdims=True)
    y = o_t * lax.rsqrt(ms + EPS) * w_ref[...] * (1.0 - LAM_INIT)
    o_ref[...] = y.T.astype(o_ref.dtype)


def _diff_attn(qh, kh, vt, lam_vecs, onorm_w, *, n_batch, seq, ctx_len, tv, tq, tk, group):
    assert ctx_len % tv == 0 and tk % tv == 0 and seq % tk == 0 and seq % tq == 0 and ctx_len <= tk
    blocks = ((0, 0, ctx_len),) + tuple((1, j * tk, tk) for j in range(seq // tk))
    nq = seq // tq
    ctx0 = n_batch * seq // ctx_len
    return pl.pallas_call(
        functools.partial(_diff_attn_kernel, tq=tq, tv=tv, blocks=blocks, group=group),
        grid=(n_batch, DIFF_HEADS, nq),
        in_specs=[pl.BlockSpec((tq, LANES), lambda b, h, i: (b * nq + i, h)),
                  pl.BlockSpec((ctx_len, LANES), lambda b, h, i: (ctx0 + b, h)),
                  pl.BlockSpec((seq, LANES), lambda b, h, i: (b, h)),
                  pl.BlockSpec((ctx_len // tv, None, DIFF_DV, tv), lambda b, h, i: (ctx0 + b, h, 0, 0)),
                  pl.BlockSpec((seq // tv, None, DIFF_DV, tv), lambda b, h, i: (b, h, 0, 0)),
                  pl.BlockSpec((4, DIFF_DQK), lambda b, h, i: (0, 0)),
                  pl.BlockSpec((DIFF_DV, 1), lambda b, h, i: (0, 0))],
        out_specs=pl.BlockSpec((tq, DIFF_DV), lambda b, h, i: (b * nq + i, h)),
        out_shape=jax.ShapeDtypeStruct((n_batch * seq, DIFF_V), BF16),
        scratch_shapes=[pltpu.VMEM((2 * tq, LANES), BF16),
                        pltpu.VMEM((2, tk, 2 * tq), BF16),
                        pltpu.VMEM((DIFF_DV, 2 * tq), F32),
                        pltpu.VMEM((1, 2 * tq), F32),
                        pltpu.VMEM((2, 1, 2 * tq), F32),
                        pltpu.VMEM((1, 2 * tq), F32)],
        compiler_params=_params(3),
        name="diff_attn",
    )(qh, kh, kh, vt, vt, lam_vecs, onorm_w.reshape(DIFF_DV, 1))


HALO = 8


def _ffn_up_kernel(x_ref, hb_ref, wg_ref, wv_ref, cwg_ref, cwv_ref, cbg_ref, cbv_ref, o_ref,
                   wgb_ref, wvb_ref, ub_ref, ug_ref, uv_ref, *, tm, seq, sub):
    i = pl.program_id(1)
    tn = o_ref.shape[1]

    @pl.when(i == 0)
    def _():
        wgb_ref[...] = wg_ref[...].astype(BF16)
        wvb_ref[...] = wv_ref[...].astype(BF16)
        hb = hb_ref[...]
        ub_ref[:, :tn] = jnp.dot(hb, wgb_ref[...], preferred_element_type=F32)
        ub_ref[:, tn:] = jnp.dot(hb, wvb_ref[...], preferred_element_type=F32)

    t0 = (i * tm) % seq
    first = t0 == 0
    last = t0 + tm == seq
    bnd = ub_ref[pl.ds(pl.multiple_of(i * HALO, HALO), HALO), :]
    streams = ((wgb_ref, ug_ref, cwg_ref, cbg_ref, 0), (wvb_ref, uv_ref, cwv_ref, cbv_ref, tn))
    for _, u_ref, _, _, c0 in streams:
        u_ref[HALO - 1:HALO, :] = jnp.where(first, 0.0, bnd[0:1, c0:c0 + tn])
        u_ref[HALO + tm:HALO + tm + 1, :] = jnp.where(last, 0.0, bnd[1:2, c0:c0 + tn])

    for wb_ref, u_ref, _, _, _ in streams:
        u_ref[HALO:HALO + tm, :] = jnp.dot(x_ref[...], wb_ref[...], preferred_element_type=F32)

    def activate(r):
        lo = HALO + r * sub
        conv = []
        for _, u_ref, cw_ref, cb_ref, _ in streams:
            cw = cw_ref[...]
            win = u_ref[lo - HALO:lo + sub + HALO, :]
            n = win.shape[0]
            before = pltpu.roll(win, 1, axis=0)[HALO:HALO + sub]
            after = pltpu.roll(win, n - 1, axis=0)[HALO:HALO + sub]
            conv.append(before * cw[0:1] + win[HALO:HALO + sub] * cw[1:2] + after * cw[2:3]
                        + cb_ref[...])
        cg, cv = conv
        o_ref[r * sub:(r + 1) * sub, :] = (cg * _sigmoid(cg) * cv).astype(o_ref.dtype)

    for r in range(tm // sub):
        activate(r)


def _ffn_up(h2, w_up, conv_w, conv_b, *, seq, tm, tn, sub):
    rows, k = h2.shape
    dff = w_up.shape[1] // 2
    nj = dff // tn
    nt = rows // tm
    tiles = h2.reshape(nt, tm, k)
    zero = jnp.zeros((1, k), h2.dtype)
    before = jnp.concatenate([zero, tiles[:-1, tm - 1]], axis=0)
    after = jnp.concatenate([tiles[1:, 0], zero], axis=0)
    hb = jnp.concatenate([before[:, None], after[:, None], jnp.zeros((nt, HALO - 2, k), h2.dtype)],
                         axis=1).reshape(nt * HALO, k)
    wsp = lambda off: pl.BlockSpec((k, tn), lambda j, i: (0, off + j))
    csp = lambda off: pl.BlockSpec((CONV_W, tn), lambda j, i: (0, off + j))
    bsp = lambda off: pl.BlockSpec((1, tn), lambda j, i: (0, off + j))
    cb = conv_b.reshape(1, 2 * dff)
    return pl.pallas_call(
        functools.partial(_ffn_up_kernel, tm=tm, seq=seq, sub=sub),
        grid=(nj, nt),
        in_specs=[pl.BlockSpec((tm, k), lambda j, i: (i, 0)),
                  pl.BlockSpec((nt * HALO, k), lambda j, i: (0, 0)),
                  wsp(0), wsp(nj), csp(0), csp(nj), bsp(0), bsp(nj)],
        out_specs=pl.BlockSpec((tm, tn), lambda j, i: (i, j)),
        out_shape=jax.ShapeDtypeStruct((rows, dff), BF16),
        scratch_shapes=[pltpu.VMEM((k, tn), BF16), pltpu.VMEM((k, tn), BF16),
                        pltpu.VMEM((nt * HALO, 2 * tn), F32),
                        pltpu.VMEM((tm + 2 * HALO, tn), F32), pltpu.VMEM((tm + 2 * HALO, tn), F32)],
        compiler_params=_params(2),
        name="ffn_up",
    )(h2, hb, w_up, w_up, conv_w, conv_w, cb, cb)


def kernel(x, c, ctx, c_ctx, w_ada, b_ada, norm1_w, w_in, w_a_up_f, b_a_f, w_a_up_b, b_a_b, gla_onorm_w, diff_qnorm_w, diff_knorm_w, lambda_q1, lambda_k1, lambda_q2, lambda_k2, diff_onorm_w, w_proj_gla, w_proj_diff, w_gate, b_gate, w_out, norm2_w, w_up, conv_w, conv_b, w_down):
    n_batch, seq, d = x.shape
    ctx_len = ctx.shape[1]
    assert w_ada.shape[0] == 1, "single-layer block"
    rows = n_batch * seq
    crow = n_batch * ctx_len

    cs = jnp.concatenate([c, c_ctx[None, :], jnp.zeros((8 - n_batch - 1, d), F32)], axis=0)
    mod = _ada(cs, w_ada[0], b_ada[0])
    mod3 = mod.reshape(8, 1, 6 * d)

    o_r = 2 * GLA_QK + 2 * GLA_V
    o_d = o_r + 2 * GLA_RANK
    o_v = o_d + 2 * DIFF_QK
    w_i = w_in[0]
    w_dv_t = w_i[:, o_v:].T
    w_lr = jnp.pad(w_i[:, o_r:o_d], ((0, 0), (0, LANES - 2 * GLA_RANK)))
    w_upcat = jnp.zeros((LANES, 2 * GLA_QK), F32)
    w_upcat = w_upcat.at[:GLA_RANK, :GLA_QK].set(w_a_up_f[0])
    w_upcat = w_upcat.at[GLA_RANK:2 * GLA_RANK, GLA_QK:].set(w_a_up_b[0]).astype(BF16)
    b_upcat = jnp.concatenate([b_a_f[0], b_a_b[0]]).reshape(1, 2 * GLA_QK)

    x2 = x.reshape(rows, d)
    t_norm = TILES["norm"]
    h_all = _norm1(x2, ctx.reshape(crow, d), norm1_w[0], mod3, seq=seq, n_batch=n_batch, tm=t_norm)
    all_rows = rows + crow
    in_tiles = dict(rows=all_rows, tm=all_rows // 8, **TILES["in_proj"])
    qkv = _mm(h_all, w_i, n_cols=o_r, name="in_proj_gla", **in_tiles)
    dqk = _mm(h_all, w_i, n_cols=2 * DIFF_QK, name="in_proj_diff", col0=o_r, shift=o_d - o_r,
              **in_tiles)
    v_t = _mm_vt(w_dv_t, h_all, tv=ctx_len, tn=t_norm)
    bcum = _decay(h_all, w_lr, w_upcat, b_upcat, tm=t_norm)

    s_zero = jnp.zeros((n_batch, GLA_HEADS // 2, 2 * GLA_DV, LANES), F32)
    seg_c = dict(row0=rows, seg_len=ctx_len, n_batch=n_batch, tb=ctx_len)
    seg_l = dict(row0=0, seg_len=seq, n_batch=n_batch, tb=TILES["gla_rows"])
    (s_cf,) = _gla(qkv, bcum, s_zero, reverse=False, write_o=False, **seg_c)
    (s_cb,) = _gla(qkv, bcum, s_zero, reverse=True, write_o=False, **seg_c)
    o_lf, _ = _gla(qkv, bcum, s_cf, reverse=False, write_o=True, o_rows=rows, **seg_l)
    y_a, _ = _gla(qkv, bcum, s_cb, reverse=True, write_o=True, o_rows=rows,
                  final=(o_lf, gla_onorm_w[0]), **seg_l)

    t_prep = t_norm
    tables = _rope_tables(seq, t_prep)
    lat_tiles = seq // t_prep
    qh = _qk_prep(dqk, diff_qnorm_w[0], tables, col0=0, rows=rows, tm=t_prep,
                  scale=(DIFF_DQK ** -0.5) * LOG2E, table_map=lambda i: i % lat_tiles)
    kh = _qk_prep(dqk, diff_knorm_w[0], tables, col0=DIFF_QK, rows=all_rows, tm=t_prep, scale=1.0,
                  table_map=lambda i: jnp.where(i < n_batch * lat_tiles, i % lat_tiles, lat_tiles))
    lam_vecs = jnp.stack([lambda_q1[0], lambda_k1[0], lambda_q2[0], lambda_k2[0]], axis=0)
    y_b = _diff_attn(qh, kh, v_t, lam_vecs, diff_onorm_w[0], n_batch=n_batch, seq=seq, ctx_len=ctx_len,
                     tv=ctx_len, **TILES["attention"])

    dense = TILES["dense"]
    g = _mm(h_all, w_gate[0], rows=rows, n_cols=2 * d, name="gate", bias=b_gate[0], **dense)
    merged = _merge(y_a, y_b, w_proj_gla[0], w_proj_diff[0], g, **dense)
    x1 = _resid_mm(merged, w_out[0], x2, mod3, gate_chunk=2, seq=seq, name="out_proj", **dense)

    h2 = _norm2(x1, norm2_w[0], mod3, seq=seq, tm=t_norm)
    act = _ffn_up(h2, w_up[0], conv_w[0], conv_b[0], seq=seq, **TILES["ffn_up"])
    out = _resid_mm(act, w_down[0], x1, mod3, gate_chunk=5, seq=seq, name="ffn_down",
                    **TILES["ffn_down"])
    return out.reshape(n_batch, seq, d)
```

```python
import functools
import math

import jax
import jax.numpy as jnp
import numpy as np
from jax import lax
from jax.experimental import pallas as pl
from jax.experimental.pallas import tpu as pltpu

F32 = jnp.float32
BF16 = jnp.bfloat16

GRID_W = 64
GLA_HEADS = 8
GLA_DK = 64
GLA_DV = 128
GLA_RANK = 16
GLA_TAU = 16.0
DIFF_HEADS = 8
DIFF_DQK = 64
DIFF_DV = 128
CONV_W = 3
ROPE_THETA = 10000.0
EPS = 1e-6
LAM_INIT = 0.8 - 0.6 * math.exp(-0.3 * 0)

GLA_QK = GLA_HEADS * GLA_DK
GLA_V = GLA_HEADS * GLA_DV
DIFF_QK = DIFF_HEADS * 2 * DIFF_DQK
DIFF_V = DIFF_HEADS * DIFF_DV

LANES = 128
V7X_VMEM_BYTES = 64 * 1024 * 1024
VMEM_LIMIT = V7X_VMEM_BYTES - 8 * 1024 * 1024

GLA_CHUNK = 64
GLA_REFS = 4
GLA_UNROLL = 32
LOG2E = 1.4426950408889634

TILES = dict(
    norm=512,
    dense=dict(tm=1024, tn=1024),
    in_proj=dict(tn=1024),
    ffn_up=dict(tm=1024, tn=512, sub=256),
    ffn_down=dict(tm=512, tn=512),
    gla_rows=2048,
    attention=dict(tq=512, tk=256, group=256),
)


def _params(n_axes):
    return pltpu.CompilerParams(dimension_semantics=("arbitrary",) * n_axes,
                                vmem_limit_bytes=VMEM_LIMIT)


def _sigmoid(x):
    return 1.0 / (1.0 + jnp.exp2(x * (-LOG2E)))


def _split_bf16(x):
    hi = x.astype(BF16)
    lo = (x - hi.astype(F32)).astype(BF16)
    return hi, lo


def _ada_kernel(c_ref, w_ref, b_ref, o_ref):
    c = c_ref[...]
    s = (c * _sigmoid(c)).astype(BF16)
    o_ref[...] = jnp.dot(s, w_ref[...].astype(BF16), preferred_element_type=F32) + b_ref[...]


def _ada(cs, w, b):
    d, n = w.shape
    tn = TILES["dense"]["tn"]
    return pl.pallas_call(
        _ada_kernel,
        grid=(n // tn,),
        in_specs=[pl.BlockSpec((8, d), lambda j: (0, 0)),
                  pl.BlockSpec((d, tn), lambda j: (0, j)),
                  pl.BlockSpec((1, tn), lambda j: (0, j))],
        out_specs=pl.BlockSpec((8, tn), lambda j: (0, j)),
        out_shape=jax.ShapeDtypeStruct((8, n), F32),
        compiler_params=_params(1),
        name="ada",
    )(cs, w, b.reshape(1, n))


NORM_ROWS = 16


def _norm_mod(x_ref, nw_ref, sh_ref, sc_ref, o_ref):
    gain = nw_ref[...] * (1.0 + sc_ref[...])
    shift = sh_ref[...]

    def body(r, carry):
        rows = pl.ds(pl.multiple_of(r * NORM_ROWS, NORM_ROWS), NORM_ROWS)
        x = x_ref[rows, :]
        ms = jnp.mean(x * x, axis=-1, keepdims=True)
        o_ref[rows, :] = (x * lax.rsqrt(ms + EPS) * gain + shift).astype(o_ref.dtype)
        return carry

    lax.fori_loop(0, x_ref.shape[0] // NORM_ROWS, body, 0, unroll=4)


def _norm1_kernel(x_ref, ctx_ref, nw_ref, sh_ref, sc_ref, o_ref, *, n_lat_tiles):
    i = pl.program_id(0)

    @pl.when(i < n_lat_tiles)
    def _():
        _norm_mod(x_ref, nw_ref, sh_ref, sc_ref, o_ref)

    @pl.when(i >= n_lat_tiles)
    def _():
        _norm_mod(ctx_ref, nw_ref, sh_ref, sc_ref, o_ref)


def _norm1(x2, ctx2, nw, mod3, *, seq, n_batch, tm):
    rows, d = x2.shape
    crow = ctx2.shape[0]
    assert crow == tm and rows % tm == 0 and seq % tm == 0
    nl = rows // tm
    per_b = seq // tm

    def mod_row(i):
        return jnp.where(i < nl, i // per_b, n_batch)

    return pl.pallas_call(
        functools.partial(_norm1_kernel, n_lat_tiles=nl),
        grid=(nl + 1,),
        in_specs=[pl.BlockSpec((tm, d), lambda i: (jnp.minimum(i, nl - 1), 0)),
                  pl.BlockSpec((tm, d), lambda i: (0, 0)),
                  pl.BlockSpec((1, d), lambda i: (0, 0)),
                  pl.BlockSpec((None, 1, d), lambda i: (mod_row(i), 0, 0)),
                  pl.BlockSpec((None, 1, d), lambda i: (mod_row(i), 0, 1))],
        out_specs=pl.BlockSpec((tm, d), lambda i: (i, 0)),
        out_shape=jax.ShapeDtypeStruct((rows + crow, d), BF16),
        compiler_params=_params(1),
        name="norm1",
    )(x2, ctx2, nw.reshape(1, d), mod3, mod3)


def _norm2_kernel(x_ref, nw_ref, sh_ref, sc_ref, o_ref):
    _norm_mod(x_ref, nw_ref, sh_ref, sc_ref, o_ref)


def _norm2(x2, nw, mod3, *, seq, tm):
    rows, d = x2.shape
    per_b = seq // tm
    return pl.pallas_call(
        _norm2_kernel,
        grid=(rows // tm,),
        in_specs=[pl.BlockSpec((tm, d), lambda i: (i, 0)),
                  pl.BlockSpec((1, d), lambda i: (0, 0)),
                  pl.BlockSpec((None, 1, d), lambda i: (i // per_b, 0, 3)),
                  pl.BlockSpec((None, 1, d), lambda i: (i // per_b, 0, 4))],
        out_specs=pl.BlockSpec((tm, d), lambda i: (i, 0)),
        out_shape=jax.ShapeDtypeStruct((rows, d), BF16),
        compiler_params=_params(1),
        name="norm2",
    )(x2, nw.reshape(1, d), mod3, mod3)


def _resident_bf16(w_ref, wb_ref):
    @pl.when(pl.program_id(1) == 0)
    def _():
        wb_ref[...] = w_ref[...].astype(BF16)
    return wb_ref[...]


def _mm_kernel(x_ref, w_ref, *rest, gate, shift):
    if shift:
        tail_ref, *rest = rest
    if gate:
        b_ref, o_ref, *scratch = rest
    else:
        o_ref, *scratch = rest
    if shift:
        @pl.when(pl.program_id(1) == 0)
        def _():
            wide = jnp.concatenate([w_ref[...], tail_ref[...]], axis=1)
            scratch[0][...] = wide[:, shift:shift + w_ref.shape[1]].astype(BF16)
        w = scratch[0][...]
    else:
        w = _resident_bf16(w_ref, scratch[0]) if scratch else w_ref[...]
    z = jnp.dot(x_ref[...], w, preferred_element_type=F32)
    if gate:
        z = _sigmoid(z + b_ref[...])
    o_ref[...] = z.astype(o_ref.dtype)


def _mm(x, w, *, rows, tm, tn, n_cols, name, bias=None, col0=0, shift=0):
    k = w.shape[0]
    gate = bias is not None
    assert col0 % tn == 0 and 0 <= shift < LANES
    in_specs = [pl.BlockSpec((tm, k), lambda j, i: (i, 0)),
                pl.BlockSpec((k, tn), lambda j, i: (0, col0 // tn + j))]
    args = [x, w]
    if shift:
        in_specs.append(pl.BlockSpec((k, LANES), lambda j, i: (0, (col0 + (j + 1) * tn) // LANES)))
        args.append(w)
    if gate:
        in_specs.append(pl.BlockSpec((1, tn), lambda j, i: (0, j)))
        args.append(bias.reshape(1, -1))
    return pl.pallas_call(
        functools.partial(_mm_kernel, gate=gate, shift=shift),
        grid=(n_cols // tn, rows // tm),
        in_specs=in_specs,
        out_specs=pl.BlockSpec((tm, tn), lambda j, i: (i, j)),
        out_shape=jax.ShapeDtypeStruct((rows, n_cols), BF16),
        scratch_shapes=[pltpu.VMEM((k, tn), BF16)] if (w.dtype != BF16 or shift) else [],
        compiler_params=_params(2),
        name=name,
    )(*args)


def _vt_kernel(w_ref, x_ref, o_ref, wb_ref):
    @pl.when(pl.program_id(0) == 0)
    def _():
        wb_ref[...] = w_ref[...].astype(BF16)

    vt = lax.dot_general(wb_ref[...], x_ref[...], (((1,), (1,)), ((), ())),
                         preferred_element_type=F32).astype(o_ref.dtype)
    n_tiles, n_heads, _, tv = o_ref.shape
    for t in range(n_tiles):
        for h in range(n_heads):
            o_ref[t, h] = vt[h * DIFF_DV:(h + 1) * DIFF_DV, t * tv:(t + 1) * tv]


def _mm_vt(wt, x, *, tv, tn):
    n_out, k = wt.shape
    rows = x.shape[0]
    blk = (tn // tv, DIFF_HEADS, DIFF_DV, tv)
    return pl.pallas_call(
        _vt_kernel,
        grid=(rows // tn,),
        in_specs=[pl.BlockSpec((n_out, k), lambda j: (0, 0)),
                  pl.BlockSpec((tn, k), lambda j: (j, 0))],
        out_specs=pl.BlockSpec(blk, lambda j: (j, 0, 0, 0)),
        out_shape=jax.ShapeDtypeStruct((rows // tv,) + blk[1:], BF16),
        scratch_shapes=[pltpu.VMEM((n_out, k), BF16)],
        compiler_params=_params(1),
        name="v_transposed",
    )(wt, x)


def _merge_kernel(ya_ref, yb_ref, wa_ref, wb_ref, ga_ref, gb_ref, o_ref, wab_ref, wbb_ref):
    pa = jnp.dot(ya_ref[...], _resident_bf16(wa_ref, wab_ref), preferred_element_type=F32)
    pb = jnp.dot(yb_ref[...], _resident_bf16(wb_ref, wbb_ref), preferred_element_type=F32)
    o_ref[...] = (ga_ref[...].astype(F32) * pa + gb_ref[...].astype(F32) * pb).astype(o_ref.dtype)


def _merge(ya, yb, wa, wb, g, *, tm, tn):
    rows, ka = ya.shape
    kb = yb.shape[1]
    n = wa.shape[1]
    nb = n // tn
    return pl.pallas_call(
        _merge_kernel,
        grid=(nb, rows // tm),
        in_specs=[pl.BlockSpec((tm, ka), lambda j, i: (i, 0)),
                  pl.BlockSpec((tm, kb), lambda j, i: (i, 0)),
                  pl.BlockSpec((ka, tn), lambda j, i: (0, j)),
                  pl.BlockSpec((kb, tn), lambda j, i: (0, j)),
                  pl.BlockSpec((tm, tn), lambda j, i: (i, j)),
                  pl.BlockSpec((tm, tn), lambda j, i: (i, nb + j))],
        out_specs=pl.BlockSpec((tm, tn), lambda j, i: (i, j)),
        out_shape=jax.ShapeDtypeStruct((rows, n), BF16),
        scratch_shapes=[pltpu.VMEM((ka, tn), BF16), pltpu.VMEM((kb, tn), BF16)],
        compiler_params=_params(2),
        name="merge",
    )(ya, yb, wa, wb, g, g)


def _resid_kernel(a_ref, w_ref, x_ref, g_ref, o_ref, wb_ref):
    part = jnp.dot(a_ref[...], _resident_bf16(w_ref, wb_ref), preferred_element_type=F32)
    o_ref[...] = x_ref[...] + g_ref[...] * part


def _resid_mm(a, w, xres, mod3, *, gate_chunk, seq, tm, tn, name):
    rows, k = a.shape
    n = w.shape[1]
    per_b = seq // tm
    cpc = n // tn
    return pl.pallas_call(
        _resid_kernel,
        grid=(n // tn, rows // tm),
        in_specs=[pl.BlockSpec((tm, k), lambda j, i: (i, 0)),
                  pl.BlockSpec((k, tn), lambda j, i: (0, j)),
                  pl.BlockSpec((tm, tn), lambda j, i: (i, j)),
                  pl.BlockSpec((None, 1, tn), lambda j, i: (i // per_b, 0, gate_chunk * cpc + j))],
        out_specs=pl.BlockSpec((tm, tn), lambda j, i: (i, j)),
        out_shape=jax.ShapeDtypeStruct((rows, n), F32),
        scratch_shapes=[pltpu.VMEM((k, tn), BF16)],
        compiler_params=_params(2),
        name=name,
    )(a, w, xres, mod3)


def _decay_kernel(h_ref, wa_ref, wup_ref, bup_ref, tri_f_ref, tri_b_ref, o_ref, *, half):
    ga = jnp.dot(h_ref[...], wa_ref[...].astype(BF16), preferred_element_type=F32)
    z = jnp.dot(ga.astype(BF16), wup_ref[...], preferred_element_type=F32) + bup_ref[...]
    la = (jnp.minimum(z, 0.0) - jnp.log(1.0 + jnp.exp(-jnp.abs(z)))) * (1.0 / GLA_TAU)
    hi, lo = _split_bf16(la)
    tt = tri_f_ref.shape[0]
    for tri_ref, sl in ((tri_f_ref, slice(0, half)), (tri_b_ref, slice(half, 2 * half))):
        tri = tri_ref[...]
        for r0 in range(0, hi.shape[0], tt):
            o_ref[r0:r0 + tt, sl] = (
                jnp.dot(tri, hi[r0:r0 + tt, sl], preferred_element_type=F32)
                + jnp.dot(tri, lo[r0:r0 + tt, sl], preferred_element_type=F32))


def _decay(h, wa, wup, bup, *, tm, tt=4 * GLA_CHUNK):
    rows, k = h.shape
    n = wup.shape[1]
    r = np.arange(tt)
    same = (r[:, None] // GLA_CHUNK) == (r[None, :] // GLA_CHUNK)
    tri_f = jnp.asarray(same & (r[None, :] <= r[:, None]), BF16)
    tri_b = jnp.asarray(same & (r[None, :] >= r[:, None]), BF16)
    return pl.pallas_call(
        functools.partial(_decay_kernel, half=n // 2),
        grid=(rows // tm,),
        in_specs=[pl.BlockSpec((tm, k), lambda i: (i, 0)),
                  pl.BlockSpec((k, LANES), lambda i: (0, 0)),
                  pl.BlockSpec((LANES, n), lambda i: (0, 0)),
                  pl.BlockSpec((1, n), lambda i: (0, 0)),
                  pl.BlockSpec((tt, tt), lambda i: (0, 0)),
                  pl.BlockSpec((tt, tt), lambda i: (0, 0))],
        out_specs=pl.BlockSpec((tm, n), lambda i: (i, 0)),
        out_shape=jax.ShapeDtypeStruct((rows, n), F32),
        compiler_params=_params(1),
        name="decay",
    )(h, wa, wup, bup, tri_f, tri_b)


def _gla_operands(q, k, b, *, reverse):
    c = GLA_CHUNK
    bs = c // GLA_REFS
    b_last = b[0:1, :] if reverse else b[c - 1:c, :]

    q_inter = q * jnp.exp(b)
    k_state = (k * jnp.exp(b_last - b)).astype(BF16)

    def rows_padded(t, lo):
        parts = [jnp.zeros((n, LANES), F32) for n in (lo,) if n] + [t]
        parts += [jnp.zeros((n, LANES), F32) for n in (c - lo - t.shape[0],) if n]
        return parts[0] if len(parts) == 1 else jnp.concatenate(parts, axis=0)

    q_segs, k_segs = [], []
    for r in range(GLA_REFS):
        lo, hi = r * bs, (r + 1) * bs
        m_r = b[lo:lo + 1, :] if reverse else b[hi - 1:hi, :]
        k_segs.append(rows_padded(k[lo:hi] * jnp.exp(m_r - b[lo:hi]), lo))
        if reverse:
            q_segs.append(rows_padded(q[:hi] * jnp.exp(b[:hi] - m_r), 0))
        else:
            q_segs.append(rows_padded(q[lo:] * jnp.exp(b[lo:] - m_r), lo))

    def per_head(t):
        ln = lax.broadcasted_iota(jnp.int32, t.shape, 1)
        ma = (ln % LANES) < GLA_DK
        return jnp.concatenate([jnp.where(ma, t, 0.0), jnp.where(ma, 0.0, t)], axis=0).astype(BF16)

    q_stack = per_head(jnp.concatenate(q_segs, axis=1))
    k_stack = jnp.concatenate(k_segs, axis=1).astype(BF16)
    return q_stack, k_stack, per_head(q_inter), k_state, jnp.exp(b_last)


def _gla_group(chunks, state_t, *, reverse):
    c = GLA_CHUNK
    dn_nt = (((1,), (1,)), ((), ()))
    ops = [_gla_operands(q, k, b, reverse=reverse) for q, k, _, b in chunks]
    scores = [lax.dot_general(o[0], o[1], dn_nt, preferred_element_type=F32) for o in ops]
    updates = [jnp.dot(ch[2].T, o[3], preferred_element_type=F32) for ch, o in zip(chunks, ops)]
    row = lax.broadcasted_iota(jnp.int32, (2 * c, c), 0) % c
    col = lax.broadcasted_iota(jnp.int32, (2 * c, c), 1)
    keep = (col >= row) if reverse else (col <= row)
    intra = []
    for ch, s in zip(chunks, scores):
        p = jnp.where(keep, s, 0.0).astype(BF16)
        v = ch[2]
        intra.append((jnp.dot(p[:c], v[:, :GLA_DV], preferred_element_type=F32),
                      jnp.dot(p[c:], v[:, GLA_DV:], preferred_element_type=F32)))
    outs = []
    for o, (ia, ib), upd in zip(ops, intra, updates):
        st = state_t.astype(BF16)
        qi = o[2]
        o_a = ia + lax.dot_general(qi[:c], st[:GLA_DV], dn_nt, preferred_element_type=F32)
        o_b = ib + lax.dot_general(qi[c:], st[GLA_DV:], dn_nt, preferred_element_type=F32)
        outs.append(jnp.concatenate([o_a, o_b], axis=1))
        state_t = o[4] * state_t + upd
    return outs, state_t


def _gla_kernel(*refs, reverse, n_chunks, unroll, write_o, final):
    q_ref, k_ref, v_ref, b_ref, s0_ref = refs[:5]
    pos = 5
    if final:
        of_ref, r_ref, nw_ref = refs[pos:pos + 3]
        pos += 3
    if write_o:
        o_ref = refs[pos]
        pos += 1
    sout_ref, state_ref = refs[pos], refs[pos + 1]
    j = pl.program_id(2)
    c = GLA_CHUNK

    @pl.when(j == 0)
    def _():
        state_ref[...] = s0_ref[...]

    def body(trip, carry):
        starts, chunks = [], []
        for u in range(unroll):
            step = trip * unroll + u
            ci = (n_chunks - 1 - step) if reverse else step
            r0 = pl.multiple_of(ci * c, c)
            starts.append(r0)
            chunks.append((q_ref[pl.ds(r0, c), :].astype(F32) * (GLA_DK ** -0.5),
                           k_ref[pl.ds(r0, c), :].astype(F32),
                           v_ref[pl.ds(r0, c), :],
                           b_ref[pl.ds(r0, c), :]))
        outs, new_state = _gla_group(chunks, state_ref[...], reverse=reverse)
        state_ref[...] = new_state
        for r0, o in zip(starts, outs):
            emit(r0, o)
        return carry

    def emit(r0, o):
        if final:
            o = o + of_ref[pl.ds(r0, c), :]
            gate = r_ref[pl.ds(r0, c), :].astype(F32)
            gate = gate * _sigmoid(gate)
            nw = nw_ref[...]
            ys = []
            for h in range(2):
                oh = o[:, h * GLA_DV:(h + 1) * GLA_DV]
                ms = jnp.mean(oh * oh, axis=-1, keepdims=True)
                ys.append(oh * lax.rsqrt(ms + EPS) * nw)
            o_ref[pl.ds(r0, c), :] = (jnp.concatenate(ys, axis=1) * gate).astype(o_ref.dtype)
        elif write_o:
            o_ref[pl.ds(r0, c), :] = o

    lax.fori_loop(0, n_chunks // unroll, body, 0)

    @pl.when(j == pl.num_programs(2) - 1)
    def _():
        sout_ref[...] = state_ref[...]


def _gla(qkv, bcum, s0, *, row0, seg_len, n_batch, tb, reverse, write_o, o_rows=0, final=None):
    n_pairs = GLA_HEADS // 2
    nblk = seg_len // tb
    assert row0 % tb == 0 and seg_len % tb == 0 and tb % GLA_CHUNK == 0
    qw, vw = LANES, 2 * GLA_DV
    k_col0 = GLA_QK // qw
    v_col0 = (2 * GLA_QK) // vw
    r_col0 = (2 * GLA_QK + GLA_V) // vw
    b_col0 = (GLA_QK // qw) if reverse else 0

    def blk(b, j):
        jj = (nblk - 1 - j) if reverse else j
        return row0 // tb + b * nblk + jj

    def oblk(b, j):
        jj = (nblk - 1 - j) if reverse else j
        return b * nblk + jj

    in_specs = [pl.BlockSpec((tb, qw), lambda b, p, j: (blk(b, j), p)),
                pl.BlockSpec((tb, qw), lambda b, p, j: (blk(b, j), k_col0 + p)),
                pl.BlockSpec((tb, vw), lambda b, p, j: (blk(b, j), v_col0 + p)),
                pl.BlockSpec((tb, qw), lambda b, p, j: (blk(b, j), b_col0 + p)),
                pl.BlockSpec((None, None, vw, qw), lambda b, p, j: (b, p, 0, 0))]
    args = [qkv, qkv, qkv, bcum, s0]
    if final is not None:
        o_fwd, onw = final
        in_specs += [pl.BlockSpec((tb, vw), lambda b, p, j: (oblk(b, j), p)),
                     pl.BlockSpec((tb, vw), lambda b, p, j: (blk(b, j), r_col0 + p)),
                     pl.BlockSpec((1, GLA_DV), lambda b, p, j: (0, 0))]
        args += [o_fwd, qkv, onw.reshape(1, GLA_DV)]
    out_specs, out_shape = [], []
    if write_o:
        out_specs.append(pl.BlockSpec((tb, vw), lambda b, p, j: (oblk(b, j), p)))
        out_shape.append(jax.ShapeDtypeStruct((o_rows, GLA_V), BF16 if final is not None else F32))
    out_specs.append(pl.BlockSpec((None, None, vw, qw), lambda b, p, j: (b, p, 0, 0)))
    out_shape.append(jax.ShapeDtypeStruct(s0.shape, F32))
    outs = pl.pallas_call(
        functools.partial(_gla_kernel, reverse=reverse, n_chunks=tb // GLA_CHUNK,
                          unroll=min(GLA_UNROLL, tb // GLA_CHUNK), write_o=write_o, final=final is not None),
        grid=(n_batch, n_pairs, nblk),
        in_specs=in_specs,
        out_specs=out_specs,
        out_shape=out_shape,
        scratch_shapes=[pltpu.VMEM((vw, qw), F32)],
        compiler_params=_params(3),
        name="gla_" + ("bwd" if reverse else "fwd") + ("_o" if write_o else "_state"),
    )(*args)
    return outs


def _exact_dot(x, m):
    hi, lo = _split_bf16(x)
    return jnp.dot(hi, m, preferred_element_type=F32) + jnp.dot(lo, m, preferred_element_type=F32)


def _qk_prep_kernel(x_ref, w_ref, g_ref, perm_ref, cos_ref, sin_ref, o_ref, *, scale, n_heads):
    g, perm = g_ref[...], perm_ref[...]
    w = w_ref[...]
    cos, sin = cos_ref[...], sin_ref[...]
    for h in range(n_heads):
        sl = slice(h * LANES, (h + 1) * LANES)
        x = x_ref[:, sl].astype(F32)
        y = x * lax.rsqrt(_exact_dot(x * x, g) + EPS) * w
        y = y * cos + _exact_dot(y, perm) * sin
        o_ref[:, sl] = (y * scale).astype(o_ref.dtype)


def _qk_prep(qkv, w, tables, *, col0, rows, tm, scale, table_map):
    n = DIFF_QK
    cos_t, sin_t = tables
    lane = np.arange(LANES)
    gmat = jnp.asarray((lane[:, None] // DIFF_DQK == lane[None, :] // DIFF_DQK) / DIFF_DQK, BF16)
    n_freq = DIFF_DQK // 4
    partner = np.where(lane % (2 * n_freq) < n_freq, lane + n_freq, lane - n_freq)
    perm = jnp.asarray(lane[:, None] == partner[None, :], BF16)
    w2 = jnp.tile(w.reshape(1, DIFF_DQK), (1, LANES // DIFF_DQK))
    tspec = pl.BlockSpec((tm, LANES), lambda i: (table_map(i), 0))
    mspec = pl.BlockSpec((LANES, LANES), lambda i: (0, 0))
    return pl.pallas_call(
        functools.partial(_qk_prep_kernel, scale=scale, n_heads=DIFF_HEADS),
        grid=(rows // tm,),
        in_specs=[pl.BlockSpec((tm, n), lambda i: (i, col0 // n)),
                  pl.BlockSpec((1, LANES), lambda i: (0, 0)),
                  mspec, mspec, tspec, tspec],
        out_specs=pl.BlockSpec((tm, n), lambda i: (i, 0)),
        out_shape=jax.ShapeDtypeStruct((rows, n), BF16),
        compiler_params=_params(1),
        name="qk_prep",
    )(qkv, w2, gmat, perm, cos_t, sin_t)


def _rope_tables(seq, ctx_rows):
    n_freq = DIFF_DQK // 4
    t = np.arange(seq)
    inv_freq = np.float32(ROPE_THETA) ** (-np.arange(n_freq, dtype=np.float32) / np.float32(n_freq))
    ang_row = (t // GRID_W).astype(np.float32)[:, None] * inv_freq
    ang_col = (t % GRID_W).astype(np.float32)[:, None] * inv_freq
    lane = np.arange(LANES)
    use_col = (lane % DIFF_DQK) >= DIFF_DQK // 2
    first_half = (lane % (2 * n_freq)) < n_freq
    ang = np.where(use_col[None, :], ang_col[:, lane % n_freq], ang_row[:, lane % n_freq])
    cos, sin = np.cos(ang).astype(np.float32), np.sin(ang).astype(np.float32)
    signed_sin = np.where(first_half[None, :], -sin, sin)
    pad = lambda a, v: np.concatenate([a, np.full((ctx_rows, LANES), v, np.float32)], axis=0)
    return jnp.asarray(pad(cos, 1.0)), jnp.asarray(pad(signed_sin, 0.0))


def _diff_attn_kernel(q_ref, kc_ref, kl_ref, vc_ref, vl_ref, lam_ref, w_ref, o_ref,
                      qs_ref, p_ref, acc_ref, m_ref, alpha_ref, l_ref, *, tq, tv, blocks, group):
    k_refs, v_refs = (kc_ref, kl_ref), (vc_ref, vl_ref)
    q = q_ref[...]
    lane = lax.broadcasted_iota(jnp.int32, q.shape, 1)
    zero = jnp.zeros_like(q)
    first = lane < DIFF_DQK
    qs_ref[...] = jnp.concatenate([jnp.where(first, q, zero), jnp.where(first, zero, q)], axis=0)
    width = 2 * tq

    def probabilities(blk, slot):
        seg, k0, nk = blk
        kblk = k_refs[seg][k0:k0 + nk, :]
        for g0 in range(0, width, group):
            cols = slice(g0, g0 + group)
            s = lax.dot_general(kblk, qs_ref[cols, :], (((1,), (1,)), ((), ())),
                                preferred_element_type=F32).astype(BF16)
            m_old = m_ref[:, cols]
            m_new = jnp.maximum(m_old, jnp.max(s, axis=0, keepdims=True).astype(F32))
            m_ref[:, cols] = m_new
            alpha = jnp.exp2(m_old - m_new)
            alpha_ref[slot, :, cols] = alpha
            p = jnp.exp2(s - m_new.astype(BF16))
            p_ref[slot, :nk, cols] = p
            l_ref[:, cols] = alpha * l_ref[:, cols] + jnp.sum(p.astype(F32), axis=0, keepdims=True)

    def values(blk, slot):
        seg, k0, nk = blk
        pv = None
        for t in range(nk // tv):
            part = jnp.dot(v_refs[seg][k0 // tv + t], p_ref[slot, t * tv:(t + 1) * tv, :],
                           preferred_element_type=F32)
            pv = part if pv is None else pv + part
        acc_ref[...] = alpha_ref[slot] * acc_ref[...] + pv

    acc_ref[...] = jnp.zeros_like(acc_ref)
    l_ref[...] = jnp.zeros_like(l_ref)
    m_ref[...] = jnp.full(m_ref.shape, -1e30, F32)
    probabilities(blocks[0], 0)
    for n in range(1, len(blocks)):
        probabilities(blocks[n], n % 2)
        values(blocks[n - 1], (n - 1) % 2)
    values(blocks[-1], (len(blocks) - 1) % 2)

    lam_v = lam_ref[...]
    lam = (jnp.exp(jnp.sum(lam_v[0:1] * lam_v[1:2], keepdims=True))
           - jnp.exp(jnp.sum(lam_v[2:3] * lam_v[3:4], keepdims=True)) + LAM_INIT)
    acc = acc_ref[...]
    l = l_ref[...]
    o_t = acc[:, :tq] / l[:, :tq] - lam * (acc[:, tq:] / l[:, tq:])
    ms = jnp.mean(o_t * o_t, axis=0, keepdims=True)
    y = o_t * lax.rsqrt(ms + EPS) * w_ref[...] * (1.0 - LAM_INIT)
    o_ref[...] = y.T.astype(o_ref.dtype)


def _diff_attn(qh, kh, vt, lam_vecs, onorm_w, *, n_batch, seq, ctx_len, tv, tq, tk, group):
    assert ctx_len % tv == 0 and tk % tv == 0 and seq % tk == 0 and seq % tq == 0 and ctx_len <= tk
    blocks = ((0, 0, ctx_len),) + tuple((1, j * tk, tk) for j in range(seq // tk))
    nq = seq // tq
    ctx0 = n_batch * seq // ctx_len
    return pl.pallas_call(
        functools.partial(_diff_attn_kernel, tq=tq, tv=tv, blocks=blocks, group=group),
        grid=(n_batch, DIFF_HEADS, nq),
        in_specs=[pl.BlockSpec((tq, LANES), lambda b, h, i: (b * nq + i, h)),
                  pl.BlockSpec((ctx_len, LANES), lambda b, h, i: (ctx0 + b, h)),
                  pl.BlockSpec((seq, LANES), lambda b, h, i: (b, h)),
                  pl.BlockSpec((ctx_len // tv, None, DIFF_DV, tv), lambda b, h, i: (ctx0 + b, h, 0, 0)),
                  pl.BlockSpec((seq // tv, None, DIFF_DV, tv), lambda b, h, i: (b, h, 0, 0)),
                  pl.BlockSpec((4, DIFF_DQK), lambda b, h, i: (0, 0)),
                  pl.BlockSpec((DIFF_DV, 1), lambda b, h, i: (0, 0))],
        out_specs=pl.BlockSpec((tq, DIFF_DV), lambda b, h, i: (b * nq + i, h)),
        out_shape=jax.ShapeDtypeStruct((n_batch * seq, DIFF_V), BF16),
        scratch_shapes=[pltpu.VMEM((2 * tq, LANES), BF16),
                        pltpu.VMEM((2, tk, 2 * tq), BF16),
                        pltpu.VMEM((DIFF_DV, 2 * tq), F32),
                        pltpu.VMEM((1, 2 * tq), F32),
                        pltpu.VMEM((2, 1, 2 * tq), F32),
                        pltpu.VMEM((1, 2 * tq), F32)],
        compiler_params=_params(3),
        name="diff_attn",
    )(qh, kh, kh, vt, vt, lam_vecs, onorm_w.reshape(DIFF_DV, 1))


HALO = 8


def _ffn_up_kernel(x_ref, hb_ref, wg_ref, wv_ref, cwg_ref, cwv_ref, cbg_ref, cbv_ref, o_ref,
                   wgb_ref, wvb_ref, ub_ref, ug_ref, uv_ref, *, tm, seq, sub):
    i = pl.program_id(1)
    tn = o_ref.shape[1]

    @pl.when(i == 0)
    def _():
        wgb_ref[...] = wg_ref[...].astype(BF16)
        wvb_ref[...] = wv_ref[...].astype(BF16)
        hb = hb_ref[...]
        ub_ref[:, :tn] = jnp.dot(hb, wgb_ref[...], preferred_element_type=F32)
        ub_ref[:, tn:] = jnp.dot(hb, wvb_ref[...], preferred_element_type=F32)

    t0 = (i * tm) % seq
    first = t0 == 0
    last = t0 + tm == seq
    bnd = ub_ref[pl.ds(pl.multiple_of(i * HALO, HALO), HALO), :]
    streams = ((wgb_ref, ug_ref, cwg_ref, cbg_ref, 0), (wvb_ref, uv_ref, cwv_ref, cbv_ref, tn))
    for _, u_ref, _, _, c0 in streams:
        u_ref[HALO - 1:HALO, :] = jnp.where(first, 0.0, bnd[0:1, c0:c0 + tn])
        u_ref[HALO + tm:HALO + tm + 1, :] = jnp.where(last, 0.0, bnd[1:2, c0:c0 + tn])

    for wb_ref, u_ref, _, _, _ in streams:
        u_ref[HALO:HALO + tm, :] = jnp.dot(x_ref[...], wb_ref[...], preferred_element_type=F32)

    def activate(r):
        lo = HALO + r * sub
        conv = []
        for _, u_ref, cw_ref, cb_ref, _ in streams:
            cw = cw_ref[...]
            win = u_ref[lo - HALO:lo + sub + HALO, :]
            n = win.shape[0]
            before = pltpu.roll(win, 1, axis=0)[HALO:HALO + sub]
            after = pltpu.roll(win, n - 1, axis=0)[HALO:HALO + sub]
            conv.append(before * cw[0:1] + win[HALO:HALO + sub] * cw[1:2] + after * cw[2:3]
                        + cb_ref[...])
        cg, cv = conv
        o_ref[r * sub:(r + 1) * sub, :] = (cg * _sigmoid(cg) * cv).astype(o_ref.dtype)

    for r in range(tm // sub):
        activate(r)


def _ffn_up(h2, w_up, conv_w, conv_b, *, seq, tm, tn, sub):
    rows, k = h2.shape
    dff = w_up.shape[1] // 2
    nj = dff // tn
    nt = rows // tm
    tiles = h2.reshape(nt, tm, k)
    zero = jnp.zeros((1, k), h2.dtype)
    before = jnp.concatenate([zero, tiles[:-1, tm - 1]], axis=0)
    after = jnp.concatenate([tiles[1:, 0], zero], axis=0)
    hb = jnp.concatenate([before[:, None], after[:, None], jnp.zeros((nt, HALO - 2, k), h2.dtype)],
                         axis=1).reshape(nt * HALO, k)
    wsp = lambda off: pl.BlockSpec((k, tn), lambda j, i: (0, off + j))
    csp = lambda off: pl.BlockSpec((CONV_W, tn), lambda j, i: (0, off + j))
    bsp = lambda off: pl.BlockSpec((1, tn), lambda j, i: (0, off + j))
    cb = conv_b.reshape(1, 2 * dff)
    return pl.pallas_call(
        functools.partial(_ffn_up_kernel, tm=tm, seq=seq, sub=sub),
        grid=(nj, nt),
        in_specs=[pl.BlockSpec((tm, k), lambda j, i: (i, 0)),
                  pl.BlockSpec((nt * HALO, k), lambda j, i: (0, 0)),
                  wsp(0), wsp(nj), csp(0), csp(nj), bsp(0), bsp(nj)],
        out_specs=pl.BlockSpec((tm, tn), lambda j, i: (i, j)),
        out_shape=jax.ShapeDtypeStruct((rows, dff), BF16),
        scratch_shapes=[pltpu.VMEM((k, tn), BF16), pltpu.VMEM((k, tn), BF16),
                        pltpu.VMEM((nt * HALO, 2 * tn), F32),
                        pltpu.VMEM((tm + 2 * HALO, tn), F32), pltpu.VMEM((tm + 2 * HALO, tn), F32)],
        compiler_params=_params(2),
        name="ffn_up",
    )(h2, hb, w_up, w_up, conv_w, conv_w, cb, cb)


def kernel(x, c, ctx, c_ctx, w_ada, b_ada, norm1_w, w_in, w_a_up_f, b_a_f, w_a_up_b, b_a_b, gla_onorm_w, diff_qnorm_w, diff_knorm_w, lambda_q1, lambda_k1, lambda_q2, lambda_k2, diff_onorm_w, w_proj_gla, w_proj_diff, w_gate, b_gate, w_out, norm2_w, w_up, conv_w, conv_b, w_down):
    n_batch, seq, d = x.shape
    ctx_len = ctx.shape[1]
    assert w_ada.shape[0] == 1, "single-layer block"
    rows = n_batch * seq
    crow = n_batch * ctx_len

    cs = jnp.concatenate([c, c_ctx[None, :], jnp.zeros((8 - n_batch - 1, d), F32)], axis=0)
    mod = _ada(cs, w_ada[0], b_ada[0])
    mod3 = mod.reshape(8, 1, 6 * d)

    o_r = 2 * GLA_QK + 2 * GLA_V
    o_d = o_r + 2 * GLA_RANK
    o_v = o_d + 2 * DIFF_QK
    w_i = w_in[0]
    w_dv_t = w_i[:, o_v:].T
    w_lr = jnp.pad(w_i[:, o_r:o_d], ((0, 0), (0, LANES - 2 * GLA_RANK)))
    w_upcat = jnp.zeros((LANES, 2 * GLA_QK), F32)
    w_upcat = w_upcat.at[:GLA_RANK, :GLA_QK].set(w_a_up_f[0])
    w_upcat = w_upcat.at[GLA_RANK:2 * GLA_RANK, GLA_QK:].set(w_a_up_b[0]).astype(BF16)
    b_upcat = jnp.concatenate([b_a_f[0], b_a_b[0]]).reshape(1, 2 * GLA_QK)

    x2 = x.reshape(rows, d)
    t_norm = TILES["norm"]
    h_all = _norm1(x2, ctx.reshape(crow, d), norm1_w[0], mod3, seq=seq, n_batch=n_batch, tm=t_norm)
    all_rows = rows + crow
    in_tiles = dict(rows=all_rows, tm=all_rows // 8, **TILES["in_proj"])
    qkv = _mm(h_all, w_i, n_cols=o_r, name="in_proj_gla", **in_tiles)
    dqk = _mm(h_all, w_i, n_cols=2 * DIFF_QK, name="in_proj_diff", col0=o_r, shift=o_d - o_r,
              **in_tiles)
    v_t = _mm_vt(w_dv_t, h_all, tv=ctx_len, tn=t_norm)
    bcum = _decay(h_all, w_lr, w_upcat, b_upcat, tm=t_norm)

    s_zero = jnp.zeros((n_batch, GLA_HEADS // 2, 2 * GLA_DV, LANES), F32)
    seg_c = dict(row0=rows, seg_len=ctx_len, n_batch=n_batch, tb=ctx_len)
    seg_l = dict(row0=0, seg_len=seq, n_batch=n_batch, tb=TILES["gla_rows"])
    (s_cf,) = _gla(qkv, bcum, s_zero, reverse=False, write_o=False, **seg_c)
    (s_cb,) = _gla(qkv, bcum, s_zero, reverse=True, write_o=False, **seg_c)
    o_lf, _ = _gla(qkv, bcum, s_cf, reverse=False, write_o=True, o_rows=rows, **seg_l)
    y_a, _ = _gla(qkv, bcum, s_cb, reverse=True, write_o=True, o_rows=rows,
                  final=(o_lf, gla_onorm_w[0]), **seg_l)

    t_prep = t_norm
    tables = _rope_tables(seq, t_prep)
    lat_tiles = seq // t_prep
    qh = _qk_prep(dqk, diff_qnorm_w[0], tables, col0=0, rows=rows, tm=t_prep,
                  scale=(DIFF_DQK ** -0.5) * LOG2E, table_map=lambda i: i % lat_tiles)
    kh = _qk_prep(dqk, diff_knorm_w[0], tables, col0=DIFF_QK, rows=all_rows, tm=t_prep, scale=1.0,
                  table_map=lambda i: jnp.where(i < n_batch * lat_tiles, i % lat_tiles, lat_tiles))
    lam_vecs = jnp.stack([lambda_q1[0], lambda_k1[0], lambda_q2[0], lambda_k2[0]], axis=0)
    y_b = _diff_attn(qh, kh, v_t, lam_vecs, diff_onorm_w[0], n_batch=n_batch, seq=seq, ctx_len=ctx_len,
                     tv=ctx_len, **TILES["attention"])

    dense = TILES["dense"]
    g = _mm(h_all, w_gate[0], rows=rows, n_cols=2 * d, name="gate", bias=b_gate[0], **dense)
    merged = _merge(y_a, y_b, w_proj_gla[0], w_proj_diff[0], g, **dense)
    x1 = _resid_mm(merged, w_out[0], x2, mod3, gate_chunk=2, seq=seq, name="out_proj", **dense)

    h2 = _norm2(x1, norm2_w[0], mod3, seq=seq, tm=t_norm)
    act = _ffn_up(h2, w_up[0], conv_w[0], conv_b[0], seq=seq, **TILES["ffn_up"])
    out = _resid_mm(act, w_down[0], x1, mod3, gate_chunk=5, seq=seq, name="ffn_down",
                    **TILES["ffn_down"])
    return out.reshape(n_batch, seq, d)
```
